```python
import jax, jax.numpy as jnp
from jax import lax
import numpy as np

D_MODEL = 1024
BATCH = 8
SEQ = 2048
DEPTH = 4

GRID_W = 64
CTX_LEN = 256
D_FF = 2816
N_MOD = 9
EPS = 1e-6
ROPE_THETA = 10000.0
BLOCK_Q = 128

GQA_HEADS = 6
GQA_KV_HEADS = 2
GQA_HEAD_DIM = 64
GQA_Q = GQA_HEADS * GQA_HEAD_DIM
GQA_KV = GQA_KV_HEADS * GQA_HEAD_DIM
GQA_IN = GQA_Q + 2 * GQA_KV
CONV_DIM = 256
CONV_GROUPS = 4
CONV_WIDTH = 3
CONV_IN = 3 * CONV_DIM
MLA_HEADS = 6
MLA_Q_LORA = 384
MLA_KV_LORA = 256
MLA_NOPE = 64
MLA_ROPE = 32
MLA_V = 64
MLA_IN = MLA_Q_LORA + MLA_KV_LORA + MLA_ROPE

D_MIX = GQA_Q + CONV_DIM + MLA_HEADS * MLA_V
IN_COLS = GQA_IN + CONV_IN + MLA_IN

kernel_name = "hybrid_gqa_conv_mla_macaron_dit"


def rms_norm(x, g):
    xf = x.astype(jnp.float32)
    y = xf * lax.rsqrt(jnp.mean(xf * xf, axis=-1, keepdims=True) + EPS)
    return (y * g.astype(jnp.float32)).astype(x.dtype)


def modulate(h, shift, scale):
    return h * (1 + scale) + shift


def swiglu(h, wg, wu, wd):
    return (jax.nn.silu(h @ wg) * (h @ wu)) @ wd


def half_ffn(h, g, shift, scale, gate, wg, wu, wd):
    return h + 0.5 * gate * swiglu(modulate(rms_norm(h, g), shift, scale), wg, wu, wd)


def rope_tables(row, col, dim):
    half = dim // 2
    inv = 1.0 / (ROPE_THETA ** (jnp.arange(0, half, 2, dtype=jnp.float32) / half))
    ar = row[:, None] * inv[None, :]
    ac = col[:, None] * inv[None, :]
    ang = jnp.concatenate([ar, ar, ac, ac], axis=-1)
    return jnp.cos(ang), jnp.sin(ang)


def apply_rope(x, cos, sin):
    x0, x1, x2, x3 = jnp.split(x, 4, axis=-1)
    rot = jnp.concatenate([-x1, x0, -x3, x2], axis=-1)
    return (x * cos + rot * sin).astype(x.dtype)


def block_attention(q, k, v):
    b, hk, g, sq, dk = q.shape
    nb = sq // BLOCK_Q
    scale = dk ** -0.5
    qb = jnp.moveaxis(q.reshape(b, hk, g, nb, BLOCK_Q, dk), 3, 0)

    def one_block(qi):
        s = jnp.einsum('bhgqd,bhkd->bhgqk', qi, k).astype(jnp.float32) * scale
        p = jax.nn.softmax(s, axis=-1).astype(v.dtype)
        return jnp.einsum('bhgqk,bhkd->bhgqd', p, v)

    o = lax.map(one_block, qb)
    return jnp.moveaxis(o, 0, 3).reshape(b, hk, g, sq, v.shape[-1])


def merge_heads(o):
    b, hk, g, s, d = o.shape
    return o.transpose(0, 3, 1, 2, 4).reshape(b, s, hk * g * d)


def gqa_heads(p, g_q, g_k):
    b, s, _ = p.shape
    q, k, v = jnp.split(p, [GQA_Q, GQA_Q + GQA_KV], axis=-1)
    q = q.reshape(b, s, GQA_KV_HEADS, GQA_HEADS // GQA_KV_HEADS, GQA_HEAD_DIM).transpose(0, 2, 3, 1, 4)
    k = k.reshape(b, s, GQA_KV_HEADS, GQA_HEAD_DIM).transpose(0, 2, 1, 3)
    v = v.reshape(b, s, GQA_KV_HEADS, GQA_HEAD_DIM).transpose(0, 2, 1, 3)
    return rms_norm(q, g_q), rms_norm(k, g_k), v


def gqa_mix(p_lat, p_ctx, g_q, g_k, cos, sin, need_ctx):
    q_l, k_l, v_l = gqa_heads(p_lat, g_q, g_k)
    q_c, k_c, v_c = gqa_heads(p_ctx, g_q, g_k)
    q_l = apply_rope(q_l, cos, sin)
    k_l = apply_rope(k_l, cos, sin)
    o_l = block_attention(q_l, jnp.concatenate([k_c, k_l], axis=2), jnp.concatenate([v_c, v_l], axis=2))
    o_c = merge_heads(block_attention(q_c, k_c, v_c)) if need_ctx else None
    return merge_heads(o_l), o_c


def short_conv(p, w, bias):
    x_in, b_gate, c_gate = jnp.split(p, 3, axis=-1)
    u = c_gate * x_in
    s = u.shape[1]
    pad = CONV_WIDTH // 2
    up = jnp.pad(u, ((0, 0), (pad, pad), (0, 0)))
    y = sum(up[:, j:j + s] * w[j] for j in range(CONV_WIDTH)) + bias
    return b_gate * y


def mla_heads(p, g_cq, g_ckv, w_uq, w_ukv, g_qn, g_kn, g_qr, g_kr):
    b, s, _ = p.shape
    cq, ckv, kr = jnp.split(p, [MLA_Q_LORA, MLA_Q_LORA + MLA_KV_LORA], axis=-1)
    q = (rms_norm(cq, g_cq) @ w_uq).reshape(b, s, MLA_HEADS, MLA_NOPE + MLA_ROPE).transpose(0, 2, 1, 3)
    kv = (rms_norm(ckv, g_ckv) @ w_ukv).reshape(b, s, MLA_HEADS, MLA_NOPE + MLA_V).transpose(0, 2, 1, 3)
    q_nope, q_rope = jnp.split(q, [MLA_NOPE], axis=-1)
    k_nope, v = jnp.split(kv, [MLA_NOPE], axis=-1)
    q_nope = rms_norm(q_nope, g_qn)
    q_rope = rms_norm(q_rope, g_qr)
    k_nope = rms_norm(k_nope, g_kn)
    k_rope = rms_norm(kr, g_kr)[:, None]
    return q_nope, q_rope, k_nope, k_rope, v


def mla_assemble(q_nope, q_rope, k_nope, k_rope):
    q = jnp.concatenate([q_nope, q_rope], axis=-1)[:, :, None]
    k = jnp.concatenate([k_nope, jnp.broadcast_to(k_rope, k_nope.shape[:-1] + (MLA_ROPE,))], axis=-1)
    return q, k


def mla_mix(p_lat, p_ctx, g_cq, g_ckv, w_uq, w_ukv, g_qn, g_kn, g_qr, g_kr, cos, sin, need_ctx):
    qn_l, qr_l, kn_l, kr_l, v_l = mla_heads(p_lat, g_cq, g_ckv, w_uq, w_ukv, g_qn, g_kn, g_qr, g_kr)
    qn_c, qr_c, kn_c, kr_c, v_c = mla_heads(p_ctx, g_cq, g_ckv, w_uq, w_ukv, g_qn, g_kn, g_qr, g_kr)
    q_l, k_l = mla_assemble(qn_l, apply_rope(qr_l, cos, sin), kn_l, apply_rope(kr_l, cos, sin))
    q_c, k_c = mla_assemble(qn_c, qr_c, kn_c, kr_c)
    o_l = block_attention(q_l, jnp.concatenate([k_c, k_l], axis=2), jnp.concatenate([v_c, v_l], axis=2))
    o_c = merge_heads(block_attention(q_c, k_c, v_c)) if need_ctx else None
    return merge_heads(o_l), o_c


def token_mixing(h_lat, h_ctx, w_in, w_out, gqa_g_q, gqa_g_k, conv_w, conv_b,
                 mla_g_cq, mla_g_ckv, mla_w_uq, mla_w_ukv, mla_g_qn, mla_g_kn, mla_g_qr, mla_g_kr,
                 cos_a, sin_a, cos_m, sin_m, need_ctx):
    splits = [GQA_IN, GQA_IN + CONV_IN]
    a_l, s_l, m_l = jnp.split(h_lat @ w_in, splits, axis=-1)
    a_c, s_c, m_c = jnp.split(h_ctx @ w_in, splits, axis=-1)
    ya_l, ya_c = gqa_mix(a_l, a_c, gqa_g_q, gqa_g_k, cos_a, sin_a, need_ctx)
    ym_l, ym_c = mla_mix(m_l, m_c, mla_g_cq, mla_g_ckv, mla_w_uq, mla_w_ukv,
                         mla_g_qn, mla_g_kn, mla_g_qr, mla_g_kr, cos_m, sin_m, need_ctx)
    y_l = jnp.concatenate([ya_l, short_conv(s_l, conv_w, conv_b), ym_l], axis=-1) @ w_out
    y_c = None
    if need_ctx:
        y_c = jnp.concatenate([ya_c, short_conv(s_c, conv_w, conv_b), ym_c], axis=-1) @ w_out
    return y_l, y_c


def setup_inputs(seed: int = 0) -> dict:
    key = jax.random.key(seed)
    ks = list(jax.random.split(key, 32))
    it = iter(ks)
    d = D_MODEL

    def nrm(shape, std):
        return std * jax.random.normal(next(it), shape, jnp.float32)

    return {
        "x": nrm((BATCH, SEQ, d), 1.0),
        "c": nrm((BATCH, d), 1.0),
        "ctx": nrm((BATCH, CTX_LEN, d), 1.0),
        "c_ctx": nrm((d,), 1.0),
        "w_mod": nrm((DEPTH, d, N_MOD * d), 0.5 * d ** -0.5),
        "b_mod": nrm((DEPTH, N_MOD * d), 0.02),
        "g_norm": 1.0 + nrm((DEPTH, 3, d), 0.02),
        "ffn_w_gate": nrm((DEPTH, 2, d, D_FF), d ** -0.5),
        "ffn_w_up": nrm((DEPTH, 2, d, D_FF), d ** -0.5),
        "ffn_w_down": nrm((DEPTH, 2, D_FF, d), D_FF ** -0.5),
        "w_in": nrm((DEPTH, d, IN_COLS), d ** -0.5),
        "w_out": nrm((DEPTH, D_MIX, d), D_MIX ** -0.5),
        "gqa_g_q": 1.0 + nrm((DEPTH, GQA_HEAD_DIM), 0.02),
        "gqa_g_k": 1.0 + nrm((DEPTH, GQA_HEAD_DIM), 0.02),
        "conv_w": nrm((DEPTH, CONV_WIDTH, CONV_DIM), CONV_WIDTH ** -0.5),
        "conv_b": nrm((DEPTH, CONV_DIM), 0.02),
        "mla_g_cq": 1.0 + nrm((DEPTH, MLA_Q_LORA), 0.02),
        "mla_g_ckv": 1.0 + nrm((DEPTH, MLA_KV_LORA), 0.02),
        "mla_w_uq": nrm((DEPTH, MLA_Q_LORA, MLA_HEADS * (MLA_NOPE + MLA_ROPE)), MLA_Q_LORA ** -0.5),
        "mla_w_ukv": nrm((DEPTH, MLA_KV_LORA, MLA_HEADS * (MLA_NOPE + MLA_V)), MLA_KV_LORA ** -0.5),
        "mla_g_qn": 1.0 + nrm((DEPTH, MLA_NOPE), 0.02),
        "mla_g_kn": 1.0 + nrm((DEPTH, MLA_NOPE), 0.02),
        "mla_g_qr": 1.0 + nrm((DEPTH, MLA_ROPE), 0.02),
        "mla_g_kr": 1.0 + nrm((DEPTH, MLA_ROPE), 0.02),
    }


def reference(x, c, ctx, c_ctx, w_mod, b_mod, g_norm, ffn_w_gate, ffn_w_up, ffn_w_down,
              w_in, w_out, gqa_g_q, gqa_g_k, conv_w, conv_b, mla_g_cq, mla_g_ckv,
              mla_w_uq, mla_w_ukv, mla_g_qn, mla_g_kn, mla_g_qr, mla_g_kr):
    n_lat = x.shape[1]
    rows = n_lat // GRID_W
    row = jnp.repeat(jnp.arange(rows, dtype=jnp.float32), GRID_W)
    col = jnp.tile(jnp.arange(GRID_W, dtype=jnp.float32), rows)
    cos_a, sin_a = rope_tables(row, col, GQA_HEAD_DIM)
    cos_m, sin_m = rope_tables(row, col, MLA_ROPE)
    s_lat = jax.nn.silu(c)[:, None, :]
    s_ctx = jax.nn.silu(c_ctx)
    for l in range(DEPTH):
        need_ctx = l < DEPTH - 1
        ml = jnp.split(s_lat @ w_mod[l] + b_mod[l], N_MOD, axis=-1)
        mc = jnp.split(s_ctx @ w_mod[l] + b_mod[l], N_MOD, axis=-1)
        x = half_ffn(x, g_norm[l, 0], ml[0], ml[1], ml[2], ffn_w_gate[l, 0], ffn_w_up[l, 0], ffn_w_down[l, 0])
        ctx = half_ffn(ctx, g_norm[l, 0], mc[0], mc[1], mc[2], ffn_w_gate[l, 0], ffn_w_up[l, 0], ffn_w_down[l, 0])
        h_lat = modulate(rms_norm(x, g_norm[l, 1]), ml[3], ml[4])
        h_ctx = modulate(rms_norm(ctx, g_norm[l, 1]), mc[3], mc[4])
        y_l, y_c = token_mixing(h_lat, h_ctx, w_in[l], w_out[l], gqa_g_q[l], gqa_g_k[l], conv_w[l], conv_b[l],
                                mla_g_cq[l], mla_g_ckv[l], mla_w_uq[l], mla_w_ukv[l],
                                mla_g_qn[l], mla_g_kn[l], mla_g_qr[l], mla_g_kr[l],
                                cos_a, sin_a, cos_m, sin_m, need_ctx)
        x = x + ml[5] * y_l
        x = half_ffn(x, g_norm[l, 2], ml[6], ml[7], ml[8], ffn_w_gate[l, 1], ffn_w_up[l, 1], ffn_w_down[l, 1])
        if need_ctx:
            ctx = ctx + mc[5] * y_c
            ctx = half_ffn(ctx, g_norm[l, 2], mc[6], mc[7], mc[8], ffn_w_gate[l, 1], ffn_w_up[l, 1], ffn_w_down[l, 1])
    return x
```

```python
import functools

import jax
import jax.numpy as jnp
from jax import lax
from jax.experimental import pallas as pl
from jax.experimental.pallas import tpu as pltpu

F32 = jnp.float32
BF16 = jnp.bfloat16

GRID_W = 64
N_MOD = 9
EPS = 1e-6
ROPE_THETA = 10000.0
GQA_HEADS = 6
GQA_KV_HEADS = 2
GQA_HEAD_DIM = 64
GQA_Q = GQA_HEADS * GQA_HEAD_DIM
GQA_KV = GQA_KV_HEADS * GQA_HEAD_DIM
GQA_IN = GQA_Q + 2 * GQA_KV
CONV_DIM = 256
CONV_WIDTH = 3
CONV_IN = 3 * CONV_DIM
MLA_HEADS = 6
MLA_Q_LORA = 384
MLA_KV_LORA = 256
MLA_NOPE = 64
MLA_ROPE = 32
MLA_V = 64
MLA_IN = MLA_Q_LORA + MLA_KV_LORA + MLA_ROPE

LANES = 128
SUBLANES = 8
MXU_DIM = 256
HEAD_SLOT = LANES
N_HEADS = 6
MLA_IN_PAD = 768
MOD_ROWS = 16
VMEM_LIMIT = 56 * 1024 * 1024
TOKEN_TILE = 512
Q_TILE = 512
FF_CHUNK = 512


def _dot(a, b):
    return jnp.dot(a, b, preferred_element_type=F32)


def _dot_nt(a, b):
    return lax.dot_general(a, b, (((1,), (1,)), ((), ())), preferred_element_type=F32)


def _silu(x):
    return x * jax.nn.sigmoid(x)


def _row_rms_scale(x, width):
    return lax.rsqrt(jnp.sum(x * x, axis=-1, keepdims=True) * (1.0 / width) + EPS)


def _norm_mod(x, g, shift, scale):
    y = x * _row_rms_scale(x, x.shape[-1])
    return (y * g) * (1.0 + scale) + shift


def _group_mean_sq(x, bd_ref):
    cols = []
    bd = bd_ref[...]
    for c in range(0, x.shape[-1], MXU_DIM):
        sq = x[:, c:c + MXU_DIM]
        sq = sq * sq
        hi = sq.astype(BF16)
        lo = (sq - hi.astype(F32)).astype(BF16)
        cols.append(_dot(hi, bd) + _dot(lo, bd))
    return jnp.concatenate(cols, axis=-1)


def _rope_cols(x, tab_ref, quarter):
    cos, sin_up, sin_dn = tab_ref[0], tab_ref[1], tab_ref[2]
    cols = []
    for c in range(0, x.shape[-1], LANES):
        v = x[:, c:c + LANES]
        up = pltpu.roll(v, LANES - quarter, axis=1)
        dn = pltpu.roll(v, quarter, axis=1)
        cols.append(v * cos + up * sin_up + dn * sin_dn)
    return cols


def _mod_kernel(c_ref, w_ref, b_ref, o_ref):
    s = _silu(c_ref[...]).astype(BF16)
    o_ref[...] = _dot(s, w_ref[...].astype(BF16)) + b_ref[...]


def _modulation(cvec, w_mod, b_mod):
    depth, d, n = w_mod.shape
    tn = 1024
    return pl.pallas_call(
        _mod_kernel,
        out_shape=jax.ShapeDtypeStruct((depth, MOD_ROWS, n), F32),
        grid=(depth, n // tn),
        in_specs=[
            pl.BlockSpec((MOD_ROWS, d), lambda l, j: (0, 0)),
            pl.BlockSpec((None, d, tn), lambda l, j: (l, 0, j)),
            pl.BlockSpec((None, 1, tn), lambda l, j: (l, 0, j)),
        ],
        out_specs=pl.BlockSpec((None, MOD_ROWS, tn), lambda l, j: (l, 0, j)),
        compiler_params=pltpu.CompilerParams(
            dimension_semantics=("arbitrary", "arbitrary"), vmem_limit_bytes=VMEM_LIMIT),
        name="modulation",
    )(cvec, w_mod, b_mod.reshape(depth, 1, n))


class _Tiling:
    def __init__(self, batch, ctx_len, seq):
        self.batch, self.ctx_len, self.seq = batch, ctx_len, seq
        self.n_ctx_rows = batch * ctx_len
        self.n_rows = self.n_ctx_rows + batch * seq
        tm = TOKEN_TILE
        while self.n_ctx_rows % tm or seq % tm:
            tm //= 2
        assert tm >= SUBLANES
        self.tm = tm
        self.ctx_tiles = self.n_ctx_rows // tm
        self.lat_tiles = batch * seq // tm
        self.tiles_per_seq = seq // tm
        assert self.n_ctx_rows % seq == 0

    def mod_row(self, i):
        return jnp.where(i < self.ctx_tiles, 0, 1 + (i - self.ctx_tiles) // self.tiles_per_seq)

    def rope_block(self, i):
        return jnp.where(i < self.ctx_tiles, 0, 1 + (i - self.ctx_tiles) % self.tiles_per_seq)


def _mod_spec(tl, layer, first_tile, d):
    return pl.BlockSpec((None, None, N_MOD, d), lambda i: (layer, tl.mod_row(i + first_tile), 0, 0))


def _layer_vec_spec(layer, width):
    return pl.BlockSpec((None, 1, width), lambda i: (layer, 0, 0))


def _resident(shape, layer):
    nd = len(shape)
    return pl.BlockSpec((None,) + tuple(shape), lambda i: (layer,) + (0,) * nd,
                        pipeline_mode=pl.Buffered(1))


def _ffn_body(x, g, mod_ref, base, wg_ref, wu_ref, wd_ref):
    h = _norm_mod(x, g, mod_ref[base:base + 1, :], mod_ref[base + 1:base + 2, :]).astype(BF16)
    d_ff = wg_ref.shape[-1]
    acc = jnp.zeros(x.shape, F32)
    for c0 in range(0, d_ff, FF_CHUNK):
        c1 = min(c0 + FF_CHUNK, d_ff)
        gate = _dot(h, wg_ref[:, c0:c1])
        up = _dot(h, wu_ref[:, c0:c1])
        act = (_silu(gate) * up).astype(BF16)
        acc = acc + _dot(act, wd_ref[c0:c1, :])
    return x + (0.5 * mod_ref[base + 2:base + 3, :]) * acc


def _ffn_kernel(x_ref, g_ref, mod_ref, wg_ref, wu_ref, wd_ref, o_ref, *, base):
    o_ref[...] = _ffn_body(x_ref[...], g_ref[...], mod_ref, base, wg_ref, wu_ref, wd_ref)


def _ffn(tl, xs, mods, g_norm, wg, wu, wd, *, layer, which, first_tile, n_tiles):
    d = xs.shape[-1]
    d_ff = wg.shape[-1]
    base = 0 if which == 0 else 6
    gi = layer * 3 + (0 if which == 0 else 2)
    wi = layer * 2 + which
    return pl.pallas_call(
        functools.partial(_ffn_kernel, base=base),
        out_shape=jax.ShapeDtypeStruct((n_tiles * tl.tm, d), F32),
        grid=(n_tiles,),
        in_specs=[
            pl.BlockSpec((tl.tm, d), lambda i: (i, 0)),
            _layer_vec_spec(gi, d),
            _mod_spec(tl, layer, first_tile, d),
            _resident((d, d_ff), wi),
            _resident((d, d_ff), wi),
            _resident((d_ff, d), wi),
        ],
        out_specs=pl.BlockSpec((tl.tm, d), lambda i: (i, 0)),
        compiler_params=pltpu.CompilerParams(
            dimension_semantics=("arbitrary",), vmem_limit_bytes=VMEM_LIMIT),
        name="ffn",
    )(xs, g_norm, mods, wg, wu, wd)


def _mix_pre_kernel(x_ref, g_ref, mod_ref, w_in_ref, gqk_ref, gcq_ref, gckv_ref, wuq_ref, wukv_ref,
                    gqm_ref, gkm_ref, gkr_ref, bd_a_ref, bd_q_ref, bd_k_ref, tab_a_ref, tab_m_ref,
                    qa_ref, ka_ref, va_ref, u_ref, bg_ref, qm_ref, km_ref, vm_ref):
    x = x_ref[...]
    h = _norm_mod(x, g_ref[...], mod_ref[3:4, :], mod_ref[4:5, :]).astype(BF16)
    lane = lax.broadcasted_iota(jnp.int32, (x.shape[0], LANES), 1)
    low = lane < GQA_HEAD_DIM
    one_hot0 = (lane == 0).astype(F32)

    pa = _dot(h, w_in_ref[:, 0:GQA_IN])
    qk = pa[:, 0:GQA_Q + GQA_KV]
    qk = qk * lax.rsqrt(_group_mean_sq(qk, bd_a_ref) + EPS) * gqk_ref[...]
    c0, c1, c2, kcol = _rope_cols(qk, tab_a_ref, GQA_HEAD_DIM // 4)
    zero = jnp.zeros_like(c0)
    slots = [
        jnp.where(low, c0, zero),
        jnp.where(low, pltpu.roll(c0, GQA_HEAD_DIM, axis=1), zero),
        jnp.where(low, c1, zero),
        jnp.where(low, zero, c1),
        jnp.where(low, zero, pltpu.roll(c2, GQA_HEAD_DIM, axis=1)),
        jnp.where(low, zero, c2),
    ]
    qa_ref[...] = jnp.concatenate(slots, axis=-1).astype(BF16)
    ka_ref[...] = kcol.astype(BF16)
    vcol = pa[:, GQA_Q + GQA_KV:GQA_IN]
    va_ref[...] = jnp.concatenate(
        [jnp.where(low, one_hot0, pltpu.roll(vcol, GQA_HEAD_DIM, axis=1)),
         jnp.where(low, one_hot0, vcol)], axis=-1).astype(BF16)

    ps = _dot(h, w_in_ref[:, GQA_IN:GQA_IN + CONV_IN])
    u_ref[...] = ps[:, 2 * CONV_DIM:3 * CONV_DIM] * ps[:, 0:CONV_DIM]
    bg_ref[...] = ps[:, CONV_DIM:2 * CONV_DIM]

    pm = _dot(h, w_in_ref[:, GQA_IN + CONV_IN:GQA_IN + CONV_IN + MLA_IN_PAD])
    cq = pm[:, 0:MLA_Q_LORA]
    cq = (cq * _row_rms_scale(cq, MLA_Q_LORA) * gcq_ref[...]).astype(BF16)
    ckv = pm[:, MLA_Q_LORA:MLA_Q_LORA + MLA_KV_LORA]
    ckv = (ckv * _row_rms_scale(ckv, MLA_KV_LORA) * gckv_ref[...]).astype(BF16)
    q = _dot(cq, wuq_ref[...])
    q = q * lax.rsqrt(_group_mean_sq(q, bd_q_ref) + EPS) * gqm_ref[...]
    qm_ref[...] = jnp.concatenate(_rope_cols(q, tab_m_ref, MLA_ROPE // 4), axis=-1).astype(BF16)
    kr = pm[:, MLA_Q_LORA + MLA_KV_LORA:MLA_IN_PAD]
    kr = kr * _row_rms_scale(kr, MLA_ROPE) * gkr_ref[...]
    kr = _rope_cols(pltpu.roll(kr, MLA_NOPE, axis=1), tab_m_ref, MLA_ROPE // 4)[0]
    kv = _dot(ckv, wukv_ref[...])
    kn = kv * lax.rsqrt(_group_mean_sq(kv, bd_k_ref) + EPS) * gkm_ref[...]
    km_ref[...] = jnp.concatenate(
        [jnp.where(low, kn[:, c:c + LANES], kr) for c in range(0, kn.shape[-1], LANES)],
        axis=-1).astype(BF16)
    vm_ref[...] = jnp.concatenate(
        [jnp.where(low, one_hot0, kv[:, c:c + LANES]) for c in range(0, kv.shape[-1], LANES)],
        axis=-1).astype(BF16)


def _mix_pre(tl, xs, mods, g_norm, p, *, layer):
    d = xs.shape[-1]
    tm = tl.tm
    n_tiles = tl.ctx_tiles + tl.lat_tiles
    t = tl.n_rows
    hs = N_HEADS * HEAD_SLOT

    def const2(shape):
        return pl.BlockSpec(shape, lambda i: (0, 0))

    def rope_spec():
        return pl.BlockSpec((3, tm, LANES), lambda i: (0, tl.rope_block(i), 0))

    def out(width):
        return pl.BlockSpec((tm, width), lambda i: (i, 0))

    w_in = p["w_in"]
    return pl.pallas_call(
        _mix_pre_kernel,
        out_shape=(
            jax.ShapeDtypeStruct((t, hs), BF16),
            jax.ShapeDtypeStruct((t, LANES), BF16),
            jax.ShapeDtypeStruct((t, 2 * LANES), BF16),
            jax.ShapeDtypeStruct((t, CONV_DIM), F32),
            jax.ShapeDtypeStruct((t, CONV_DIM), F32),
            jax.ShapeDtypeStruct((t, hs), BF16),
            jax.ShapeDtypeStruct((t, hs), BF16),
            jax.ShapeDtypeStruct((t, hs), BF16),
        ),
        grid=(n_tiles,),
        in_specs=[
            pl.BlockSpec((tm, d), lambda i: (i, 0)),
            _layer_vec_spec(layer * 3 + 1, d),
            _mod_spec(tl, layer, 0, d),
            _resident(w_in.shape[1:], layer),
            _layer_vec_spec(layer, GQA_Q + GQA_KV),
            _layer_vec_spec(layer, MLA_Q_LORA),
            _layer_vec_spec(layer, MLA_KV_LORA),
            _resident(p["w_uq"].shape[1:], layer),
            _resident(p["w_ukv"].shape[1:], layer),
            _layer_vec_spec(layer, hs),
            _layer_vec_spec(layer, hs),
            _layer_vec_spec(layer, LANES),
            const2((MXU_DIM, MXU_DIM)),
            const2((MXU_DIM, MXU_DIM)),
            const2((MXU_DIM, MXU_DIM)),
            rope_spec(),
            rope_spec(),
        ],
        out_specs=(out(hs), out(LANES), out(2 * LANES), out(CONV_DIM), out(CONV_DIM),
                   out(hs), out(hs), out(hs)),
        compiler_params=pltpu.CompilerParams(
            dimension_semantics=("arbitrary",), vmem_limit_bytes=VMEM_LIMIT),
        name="mix_pre",
    )(xs, g_norm, mods, w_in, p["g_qk"], p["g_cq"], p["g_ckv"], p["w_uq"], p["w_ukv"],
      p["g_qm"], p["g_km"], p["g_kr"], p["bd_a"], p["bd_q"], p["bd_k"], p["tab_a"], p["tab_m"])


def _attn_kernel(*refs, n_pieces, k_slot, v_slot):
    q_ref = refs[0]
    k_refs = refs[1:1 + 2 * n_pieces:2]
    v_refs = refs[2:2 + 2 * n_pieces:2]
    o_ref = refs[1 + 2 * n_pieces]
    outs = []
    for h in range(N_HEADS):
        q = q_ref[:, h * HEAD_SLOT:(h + 1) * HEAD_SLOT]
        ks, vs = k_slot[h] * HEAD_SLOT, v_slot[h] * HEAD_SLOT
        s = [_dot_nt(q, k_ref[:, ks:ks + HEAD_SLOT]) for k_ref in k_refs]
        m = jnp.max(s[0], axis=-1, keepdims=True)
        for sp in s[1:]:
            m = jnp.maximum(m, jnp.max(sp, axis=-1, keepdims=True))
        acc = None
        for sp, v_ref in zip(s, v_refs):
            pv = _dot(jnp.exp(sp - m).astype(BF16), v_ref[:, vs:vs + HEAD_SLOT])
            acc = pv if acc is None else acc + pv
        outs.append(acc[:, MLA_V:] / acc[:, 0:1])
    o_ref[...] = jnp.concatenate(outs, axis=-1).astype(BF16)


def _attention(tl, q, k, v, *, k_slot, v_slot, latent_queries):
    b, ctx_len, seq = tl.batch, tl.ctx_len, tl.seq
    kw, vw = k.shape[-1], v.shape[-1]
    lat_block0 = tl.n_ctx_rows // seq
    if latent_queries:
        tq = min(Q_TILE, seq)
        nq = seq // tq
        q_block0 = tl.n_ctx_rows // tq
        n_rows = b * seq
        pieces = [
            pl.BlockSpec((ctx_len, kw), lambda bi, qi: (bi, 0)),
            pl.BlockSpec((ctx_len, vw), lambda bi, qi: (bi, 0)),
            pl.BlockSpec((seq, kw), lambda bi, qi: (lat_block0 + bi, 0)),
            pl.BlockSpec((seq, vw), lambda bi, qi: (lat_block0 + bi, 0)),
        ]
        operands = (q, k, v, k, v)
    else:
        tq, nq, q_block0 = ctx_len, 1, 0
        n_rows = b * ctx_len
        pieces = [
            pl.BlockSpec((ctx_len, kw), lambda bi, qi: (bi, 0)),
            pl.BlockSpec((ctx_len, vw), lambda bi, qi: (bi, 0)),
        ]
        operands = (q, k, v)
    ow = N_HEADS * MLA_V
    return pl.pallas_call(
        functools.partial(_attn_kernel, n_pieces=len(pieces) // 2, k_slot=k_slot, v_slot=v_slot),
        out_shape=jax.ShapeDtypeStruct((n_rows, ow), BF16),
        grid=(b, nq),
        in_specs=[pl.BlockSpec((tq, N_HEADS * HEAD_SLOT), lambda bi, qi: (q_block0 + bi * nq + qi, 0))]
        + pieces,
        out_specs=pl.BlockSpec((tq, ow), lambda bi, qi: (bi * nq + qi, 0)),
        compiler_params=pltpu.CompilerParams(
            dimension_semantics=("arbitrary", "arbitrary"), vmem_limit_bytes=VMEM_LIMIT),
        name="attention",
    )(*operands)


def _mod_const(v, n):
    return v & (n - 1) if n & (n - 1) == 0 else lax.rem(v, n)

def _mix_post_kernel(x_ref, mod_ref, ya_ref, ym_ref, u_ref, up_ref, un_ref, bg_ref, cw_ref, cb_ref,
                     w_out_ref, o_ref, *, tl, first_tile):
    tm = x_ref.shape[0]
    i = pl.program_id(0) + first_tile
    is_ctx = i < tl.ctx_tiles
    seq_len = jnp.where(is_ctx, tl.ctx_len, tl.seq)
    local = lax.broadcasted_iota(jnp.int32, (tm, CONV_DIM), 0)
    row = local + i * tm
    pos = jnp.where(is_ctx, _mod_const(row, tl.ctx_len), _mod_const(row, tl.seq))
    u = u_ref[...]
    prev = jnp.where(local == 0, up_ref[SUBLANES - 1:SUBLANES, :], pltpu.roll(u, 1, axis=0))
    prev = jnp.where(pos == 0, 0.0, prev)
    nxt = jnp.where(local == tm - 1, un_ref[0:1, :], pltpu.roll(u, tm - 1, axis=0))
    nxt = jnp.where(pos == seq_len - 1, 0.0, nxt)
    y = prev * cw_ref[0:1, :] + u * cw_ref[1:2, :] + nxt * cw_ref[2:3, :] + cb_ref[...]
    yc = (bg_ref[...] * y).astype(BF16)
    mixed = jnp.concatenate([ya_ref[...], yc, ym_ref[...]], axis=-1)
    o_ref[...] = x_ref[...] + mod_ref[5:6, :] * _dot(mixed, w_out_ref[...])


def _mix_post(tl, xs, mods, ya, ym, u, bg, p, *, layer, first_tile, n_tiles):
    d = xs.shape[-1]
    tm = tl.tm
    rows8 = tm // SUBLANES
    last8 = tl.n_rows // SUBLANES - 1

    def tile(width):
        return pl.BlockSpec((tm, width), lambda i: (i + first_tile, 0))

    def attn_tile(width):
        return pl.BlockSpec((tm, width), lambda i: (i, 0))

    return pl.pallas_call(
        functools.partial(_mix_post_kernel, tl=tl, first_tile=first_tile),
        out_shape=jax.ShapeDtypeStruct((n_tiles * tm, d), F32),
        grid=(n_tiles,),
        in_specs=[
            tile(d),
            _mod_spec(tl, layer, first_tile, d),
            attn_tile(GQA_Q),
            attn_tile(N_HEADS * MLA_V),
            tile(CONV_DIM),
            pl.BlockSpec((SUBLANES, CONV_DIM),
                         lambda i: (jnp.maximum((i + first_tile) * rows8 - 1, 0), 0)),
            pl.BlockSpec((SUBLANES, CONV_DIM),
                         lambda i: (jnp.minimum((i + first_tile + 1) * rows8, last8), 0)),
            tile(CONV_DIM),
            pl.BlockSpec((None, CONV_WIDTH, CONV_DIM), lambda i: (layer, 0, 0)),
            _layer_vec_spec(layer, CONV_DIM),
            _resident(p["w_out"].shape[1:], layer),
        ],
        out_specs=pl.BlockSpec((tm, d), lambda i: (i, 0)),
        compiler_params=pltpu.CompilerParams(
            dimension_semantics=("arbitrary",), vmem_limit_bytes=VMEM_LIMIT),
        name="mix_post",
    )(xs, mods, ya, ym, u, u, u, bg, p["conv_w"], p["conv_b"], p["w_out"])


def _block_diag_mean(pattern):
    m = jnp.zeros((MXU_DIM, MXU_DIM), F32)
    for base in range(0, MXU_DIM, LANES):
        o = base
        for width, is_group in pattern:
            if is_group:
                m = m.at[o:o + width, o:o + width].set(1.0 / width)
            o += width
    return m.astype(BF16)


def _rope_tables(seq, tm, head_dim, lane_offset):
    rows = seq // GRID_W
    row = jnp.repeat(jnp.arange(rows, dtype=F32), GRID_W)
    col = jnp.tile(jnp.arange(GRID_W, dtype=F32), rows)
    half = head_dim // 2
    inv = 1.0 / (ROPE_THETA ** (jnp.arange(0, half, 2, dtype=F32) / half))
    ar = row[:, None] * inv[None, :]
    ac = col[:, None] * inv[None, :]
    ang = jnp.concatenate([ar, ar, ac, ac], axis=-1)
    cos, sin = jnp.cos(ang), jnp.sin(ang)
    quarter = head_dim // 4
    up_mask = (jnp.arange(head_dim) // quarter) % 2 == 0
    sin_up = jnp.where(up_mask, -sin, 0.0)
    sin_dn = jnp.where(up_mask, 0.0, sin)
    reps = (LANES - lane_offset) // head_dim if lane_offset == 0 else 1

    def place(t, fill):
        t = jnp.tile(t, (1, reps))
        left = jnp.full((seq, lane_offset), fill, F32)
        right = jnp.zeros((seq, LANES - lane_offset - t.shape[1]), F32)
        return jnp.concatenate([left, t, right], axis=-1)

    lat = jnp.stack([place(cos, 1.0), place(sin_up, 0.0), place(sin_dn, 0.0)])
    ident = jnp.stack([jnp.ones((tm, LANES), F32), jnp.zeros((tm, LANES), F32), jnp.zeros((tm, LANES), F32)])
    if lane_offset:
        keep = (jnp.arange(LANES) < lane_offset + head_dim).astype(F32)
        ident = ident * keep
    return jnp.concatenate([ident, lat], axis=1)


def _prepare(tl, w_in, w_out, gqa_g_q, gqa_g_k, conv_w, conv_b, mla_g_cq, mla_g_ckv, mla_w_uq,
             mla_w_ukv, mla_g_qn, mla_g_kn, mla_g_qr, mla_g_kr):
    depth, d, _ = w_in.shape
    pad = MLA_IN_PAD - MLA_IN
    w_in_p = jnp.concatenate([w_in, jnp.zeros((depth, d, pad), w_in.dtype)], axis=-1).astype(BF16)
    dq = MLA_NOPE + MLA_ROPE
    wq = mla_w_uq.reshape(depth, MLA_Q_LORA, MLA_HEADS, dq)
    wq = jnp.concatenate([wq, jnp.zeros((depth, MLA_Q_LORA, MLA_HEADS, HEAD_SLOT - dq), wq.dtype)], axis=-1)
    w_uq = wq.reshape(depth, MLA_Q_LORA, MLA_HEADS * HEAD_SLOT).astype(BF16)

    def vec(a):
        return a.reshape(depth, 1, a.shape[-1]).astype(F32)

    zeros32 = jnp.zeros((depth, HEAD_SLOT - dq), F32)
    mla_scale = dq ** -0.5
    g_qm = jnp.tile(jnp.concatenate([mla_g_qn * mla_scale, mla_g_qr * mla_scale, zeros32], axis=-1),
                    (1, MLA_HEADS))
    g_km = jnp.tile(jnp.concatenate([mla_g_kn, jnp.zeros((depth, MLA_V), F32)], axis=-1), (1, MLA_HEADS))
    g_kr = jnp.concatenate([mla_g_kr, jnp.zeros((depth, LANES - MLA_ROPE), F32)], axis=-1)
    gqa_scale = GQA_HEAD_DIM ** -0.5
    g_qk = jnp.concatenate([jnp.tile(gqa_g_q * gqa_scale, (1, GQA_HEADS)),
                            jnp.tile(gqa_g_k, (1, GQA_KV_HEADS))], axis=-1)
    return {
        "w_in": w_in_p,
        "w_out": w_out.astype(BF16),
        "w_uq": w_uq,
        "w_ukv": mla_w_ukv.astype(BF16),
        "g_qk": vec(g_qk), "g_cq": vec(mla_g_cq), "g_ckv": vec(mla_g_ckv),
        "g_qm": vec(g_qm), "g_km": vec(g_km), "g_kr": vec(g_kr),
        "conv_w": conv_w.astype(F32), "conv_b": vec(conv_b),
        "bd_a": _block_diag_mean([(64, True), (64, True)]),
        "bd_q": _block_diag_mean([(MLA_NOPE, True), (MLA_ROPE, True)]),
        "bd_k": _block_diag_mean([(MLA_NOPE, True)]),
        "tab_a": _rope_tables(tl.seq, tl.tm, GQA_HEAD_DIM, 0),
        "tab_m": _rope_tables(tl.seq, tl.tm, MLA_ROPE, MLA_NOPE),
    }


GQA_K_SLOT = (0,) * N_HEADS
GQA_V_SLOT = tuple(h // (GQA_HEADS // GQA_KV_HEADS) for h in range(N_HEADS))
MLA_SLOT = tuple(range(N_HEADS))


def kernel(x, c, ctx, c_ctx, w_mod, b_mod, g_norm, ffn_w_gate, ffn_w_up, ffn_w_down, w_in, w_out,
           gqa_g_q, gqa_g_k, conv_w, conv_b, mla_g_cq, mla_g_ckv, mla_w_uq, mla_w_ukv, mla_g_qn,
           mla_g_kn, mla_g_qr, mla_g_kr):
    batch, seq, d = x.shape
    ctx_len = ctx.shape[1]
    depth = w_mod.shape[0]
    d_ff = ffn_w_gate.shape[-1]
    assert d == 1024 and batch + 1 <= MOD_ROWS and seq % GRID_W == 0
    tl = _Tiling(batch, ctx_len, seq)
    n_all = tl.ctx_tiles + tl.lat_tiles

    cvec = jnp.concatenate([c_ctx[None, :], c, jnp.zeros((MOD_ROWS - 1 - batch, d), F32)], axis=0)
    mods = _modulation(cvec, w_mod, b_mod).reshape(depth, MOD_ROWS, N_MOD, d)
    p = _prepare(tl, w_in, w_out, gqa_g_q, gqa_g_k, conv_w, conv_b, mla_g_cq, mla_g_ckv, mla_w_uq,
                 mla_w_ukv, mla_g_qn, mla_g_kn, mla_g_qr, mla_g_kr)
    g3 = g_norm.reshape(depth * 3, 1, d)
    wg = ffn_w_gate.reshape(depth * 2, d, d_ff).astype(BF16)
    wu = ffn_w_up.reshape(depth * 2, d, d_ff).astype(BF16)
    wd = ffn_w_down.reshape(depth * 2, d_ff, d).astype(BF16)

    xs = jnp.concatenate([ctx.reshape(batch * ctx_len, d), x.reshape(batch * seq, d)], axis=0)
    for layer in range(depth):
        need_ctx = layer < depth - 1
        xs = _ffn(tl, xs, mods, g3, wg, wu, wd, layer=layer, which=0, first_tile=0, n_tiles=n_all)
        qa, ka, va, u, bg, qm, km, vm = _mix_pre(tl, xs, mods, g3, p, layer=layer)
        ya = _attention(tl, qa, ka, va, k_slot=GQA_K_SLOT, v_slot=GQA_V_SLOT, latent_queries=True)
        ym = _attention(tl, qm, km, vm, k_slot=MLA_SLOT, v_slot=MLA_SLOT, latent_queries=True)
        if need_ctx:
            ya_c = _attention(tl, qa, ka, va, k_slot=GQA_K_SLOT, v_slot=GQA_V_SLOT, latent_queries=False)
            ym_c = _attention(tl, qm, km, vm, k_slot=MLA_SLOT, v_slot=MLA_SLOT, latent_queries=False)
            ya = jnp.concatenate([ya_c, ya], axis=0)
            ym = jnp.concatenate([ym_c, ym], axis=0)
            first, n_tiles = 0, n_all
        else:
            first, n_tiles = tl.ctx_tiles, tl.lat_tiles
        xs = _mix_post(tl, xs, mods, ya, ym, u, bg, p, layer=layer, first_tile=first, n_tiles=n_tiles)
        xs = _ffn(tl, xs, mods, g3, wg, wu, wd, layer=layer, which=1, first_tile=first, n_tiles=n_tiles)
    return xs.reshape(batch, seq, d)
```

```python
import functools

import jax
import jax.numpy as jnp
from jax import lax
from jax.experimental import pallas as pl
from jax.experimental.pallas import tpu as pltpu

F32 = jnp.float32
BF16 = jnp.bfloat16

GRID_W = 64
N_MOD = 9
EPS = 1e-6
ROPE_THETA = 10000.0
LOG2_E = 1.4426950408889634
GQA_HEADS = 6
GQA_KV_HEADS = 2
GQA_HEAD_DIM = 64
GQA_Q = GQA_HEADS * GQA_HEAD_DIM
GQA_KV = GQA_KV_HEADS * GQA_HEAD_DIM
GQA_IN = GQA_Q + 2 * GQA_KV
CONV_DIM = 256
CONV_WIDTH = 3
CONV_IN = 3 * CONV_DIM
MLA_HEADS = 6
MLA_Q_LORA = 384
MLA_KV_LORA = 256
MLA_NOPE = 64
MLA_ROPE = 32
MLA_V = 64
MLA_IN = MLA_Q_LORA + MLA_KV_LORA + MLA_ROPE

LANES = 128
SUBLANES = 8
MXU_DIM = 256
HEAD_SLOT = LANES
N_HEADS = 6
MLA_IN_PAD = 768
MOD_ROWS = 16
VMEM_LIMIT = 56 * 1024 * 1024
TOKEN_TILE = 512
Q_TILE = 1024
FF_CHUNK = 512


def _dot(a, b):
    return jnp.dot(a, b, preferred_element_type=F32)


def _dot_nt(a, b):
    return lax.dot_general(a, b, (((1,), (1,)), ((), ())), preferred_element_type=F32)


def _silu(x):
    return x * jax.nn.sigmoid(x)


def _row_rms_scale(x, width):
    return lax.rsqrt(jnp.sum(x * x, axis=-1, keepdims=True) * (1.0 / width) + EPS)


def _norm_mod(x, g, shift, scale):
    y = x * _row_rms_scale(x, x.shape[-1])
    return (y * g) * (1.0 + scale) + shift


def _group_mean_sq(x, bd_ref):
    cols = []
    bd = bd_ref[...]
    for c in range(0, x.shape[-1], MXU_DIM):
        sq = x[:, c:c + MXU_DIM]
        sq = sq * sq
        hi = sq.astype(BF16)
        lo = (sq - hi.astype(F32)).astype(BF16)
        cols.append(_dot(hi, bd) + _dot(lo, bd))
    return jnp.concatenate(cols, axis=-1)


def _rope_cols(x, tab_ref, quarter):
    cos, sin_up, sin_dn = tab_ref[0], tab_ref[1], tab_ref[2]
    cols = []
    for c in range(0, x.shape[-1], LANES):
        v = x[:, c:c + LANES]
        up = pltpu.roll(v, LANES - quarter, axis=1)
        dn = pltpu.roll(v, quarter, axis=1)
        cols.append(v * cos + up * sin_up + dn * sin_dn)
    return cols


def _mod_kernel(c_ref, w_ref, b_ref, o_ref):
    s = _silu(c_ref[...]).astype(BF16)
    o_ref[...] = _dot(s, w_ref[...].astype(BF16)) + b_ref[...]


def _modulation(cvec, w_mod, b_mod):
    depth, d, n = w_mod.shape
    tn = 1024
    return pl.pallas_call(
        _mod_kernel,
        out_shape=jax.ShapeDtypeStruct((depth, MOD_ROWS, n), F32),
        grid=(depth, n // tn),
        in_specs=[
            pl.BlockSpec((MOD_ROWS, d), lambda l, j: (0, 0)),
            pl.BlockSpec((None, d, tn), lambda l, j: (l, 0, j)),
            pl.BlockSpec((None, 1, tn), lambda l, j: (l, 0, j)),
        ],
        out_specs=pl.BlockSpec((None, MOD_ROWS, tn), lambda l, j: (l, 0, j)),
        compiler_params=pltpu.CompilerParams(
            dimension_semantics=("arbitrary", "arbitrary"), vmem_limit_bytes=VMEM_LIMIT),
        name="modulation",
    )(cvec, w_mod, b_mod.reshape(depth, 1, n))


class _Tiling:
    def __init__(self, batch, ctx_len, seq):
        self.batch, self.ctx_len, self.seq = batch, ctx_len, seq
        self.n_ctx_rows = batch * ctx_len
        self.n_rows = self.n_ctx_rows + batch * seq
        tm = TOKEN_TILE
        while self.n_ctx_rows % tm or seq % tm:
            tm //= 2
        assert tm >= SUBLANES
        self.tm = tm
        self.ctx_tiles = self.n_ctx_rows // tm
        self.lat_tiles = batch * seq // tm
        self.tiles_per_seq = seq // tm
        assert self.n_ctx_rows % seq == 0

    def mod_row(self, i):
        return jnp.where(i < self.ctx_tiles, 0, 1 + (i - self.ctx_tiles) // self.tiles_per_seq)

    def rope_block(self, i):
        return jnp.where(i < self.ctx_tiles, 0, 1 + (i - self.ctx_tiles) % self.tiles_per_seq)


def _mod_spec(tl, layer, first_tile, d):
    return pl.BlockSpec((None, None, N_MOD, d), lambda i: (layer, tl.mod_row(i + first_tile), 0, 0))


def _layer_vec_spec(layer, width):
    return pl.BlockSpec((None, 1, width), lambda i: (layer, 0, 0))


def _resident(shape, layer):
    nd = len(shape)
    return pl.BlockSpec((None,) + tuple(shape), lambda i: (layer,) + (0,) * nd,
                        pipeline_mode=pl.Buffered(1))


def _ffn_body(x, g, mod_ref, base, wg_ref, wu_ref, wd_ref):
    h = _norm_mod(x, g, mod_ref[base:base + 1, :], mod_ref[base + 1:base + 2, :]).astype(BF16)
    d_ff = wg_ref.shape[-1]
    acc = jnp.zeros(x.shape, F32)
    for c0 in range(0, d_ff, FF_CHUNK):
        c1 = min(c0 + FF_CHUNK, d_ff)
        gate = _dot(h, wg_ref[:, c0:c1])
        up = _dot(h, wu_ref[:, c0:c1])
        act = (_silu(gate) * up).astype(BF16)
        acc = acc + _dot(act, wd_ref[c0:c1, :])
    return x + (0.5 * mod_ref[base + 2:base + 3, :]) * acc


def _rows_specs(tl, rows, width, first_tile):
    tm = tl.tm
    if not isinstance(rows, tuple):
        return [pl.BlockSpec((tm, width), lambda i: (i, 0))]
    assert first_tile == 0
    return [pl.BlockSpec((tm, width), lambda i: (jnp.minimum(i, tl.ctx_tiles - 1), 0)),
            pl.BlockSpec((tm, width), lambda i: (jnp.maximum(i - tl.ctx_tiles, 0), 0))]


def _rows_load(tl, refs):
    if len(refs) == 1:
        return refs[0][...]
    return jnp.where(pl.program_id(0) < tl.ctx_tiles, refs[0][...], refs[1][...])


def _rows_args(rows):
    return rows if isinstance(rows, tuple) else (rows,)


def _ffn_kernel(*refs, base, tl, n_x):
    g_ref, mod_ref, wg_ref, wu_ref, wd_ref, o_ref = refs[n_x:]
    o_ref[...] = _ffn_body(_rows_load(tl, refs[:n_x]), g_ref[...], mod_ref, base, wg_ref, wu_ref, wd_ref)


def _ffn(tl, xs, mods, g_norm, wg, wu, wd, *, layer, which, first_tile, n_tiles):
    d, d_ff = wg.shape[-2:]
    base = 0 if which == 0 else 6
    gi = layer * 3 + (0 if which == 0 else 2)
    wi = layer * 2 + which
    x_args = _rows_args(xs)
    return pl.pallas_call(
        functools.partial(_ffn_kernel, base=base, tl=tl, n_x=len(x_args)),
        out_shape=jax.ShapeDtypeStruct((n_tiles * tl.tm, d), F32),
        grid=(n_tiles,),
        in_specs=_rows_specs(tl, xs, d, first_tile) + [
            _layer_vec_spec(gi, d),
            _mod_spec(tl, layer, first_tile, d),
            _resident((d, d_ff), wi),
            _resident((d, d_ff), wi),
            _resident((d_ff, d), wi),
        ],
        out_specs=pl.BlockSpec((tl.tm, d), lambda i: (i, 0)),
        compiler_params=pltpu.CompilerParams(
            dimension_semantics=("arbitrary",), vmem_limit_bytes=VMEM_LIMIT),
        name="ffn",
    )(*x_args, g_norm, mods, wg, wu, wd)


def _mix_pre_kernel(x_ref, g_ref, mod_ref, w_in_ref, gqk_ref, gcq_ref, gckv_ref, wuq_ref, wukv_ref,
                    gqm_ref, gkm_ref, gkr_ref, bd_a_ref, bd_q_ref, bd_k_ref, tab_a_ref, tab_m_ref,
                    qa_ref, ka_ref, va_ref, u_ref, bg_ref, qm_ref, km_ref, vm_ref):
    x = x_ref[...]
    h = _norm_mod(x, g_ref[...], mod_ref[3:4, :], mod_ref[4:5, :]).astype(BF16)
    lane = lax.broadcasted_iota(jnp.int32, (x.shape[0], LANES), 1)
    low = lane < GQA_HEAD_DIM
    one_hot0 = (lane == 0).astype(F32)

    pa = _dot(h, w_in_ref[:, 0:GQA_IN])
    qk = pa[:, 0:GQA_Q + GQA_KV]
    qk = qk * lax.rsqrt(_group_mean_sq(qk, bd_a_ref) + EPS) * gqk_ref[...]
    c0, c1, c2, kcol = _rope_cols(qk, tab_a_ref, GQA_HEAD_DIM // 4)
    zero = jnp.zeros_like(c0)
    slots = [
        jnp.where(low, c0, zero),
        jnp.where(low, pltpu.roll(c0, GQA_HEAD_DIM, axis=1), zero),
        jnp.where(low, c1, zero),
        jnp.where(low, zero, c1),
        jnp.where(low, zero, pltpu.roll(c2, GQA_HEAD_DIM, axis=1)),
        jnp.where(low, zero, c2),
    ]
    qa_ref[...] = jnp.concatenate(slots, axis=-1).astype(BF16)
    ka_ref[...] = kcol.astype(BF16)
    vcol = pa[:, GQA_Q + GQA_KV:GQA_IN]
    va_ref[...] = jnp.concatenate(
        [jnp.where(low, one_hot0, pltpu.roll(vcol, GQA_HEAD_DIM, axis=1)),
         jnp.where(low, one_hot0, vcol)], axis=-1).T.astype(BF16)

    ps = _dot(h, w_in_ref[:, GQA_IN:GQA_IN + CONV_IN])
    u_ref[...] = ps[:, 2 * CONV_DIM:3 * CONV_DIM] * ps[:, 0:CONV_DIM]
    bg_ref[...] = ps[:, CONV_DIM:2 * CONV_DIM]

    pm = _dot(h, w_in_ref[:, GQA_IN + CONV_IN:GQA_IN + CONV_IN + MLA_IN_PAD])
    cq = pm[:, 0:MLA_Q_LORA]
    cq = (cq * _row_rms_scale(cq, MLA_Q_LORA) * gcq_ref[...]).astype(BF16)
    ckv = pm[:, MLA_Q_LORA:MLA_Q_LORA + MLA_KV_LORA]
    ckv = (ckv * _row_rms_scale(ckv, MLA_KV_LORA) * gckv_ref[...]).astype(BF16)
    q = _dot(cq, wuq_ref[...])
    q = q * lax.rsqrt(_group_mean_sq(q, bd_q_ref) + EPS) * gqm_ref[...]
    qm_ref[...] = jnp.concatenate(_rope_cols(q, tab_m_ref, MLA_ROPE // 4), axis=-1).astype(BF16)
    kr = pm[:, MLA_Q_LORA + MLA_KV_LORA:MLA_IN_PAD]
    kr = kr * _row_rms_scale(kr, MLA_ROPE) * gkr_ref[...]
    kr = _rope_cols(pltpu.roll(kr, MLA_NOPE, axis=1), tab_m_ref, MLA_ROPE // 4)[0]
    kv = _dot(ckv, wukv_ref[...])
    kn = kv * lax.rsqrt(_group_mean_sq(kv, bd_k_ref) + EPS) * gkm_ref[...]
    km_ref[...] = jnp.concatenate(
        [jnp.where(low, kn[:, c:c + LANES], kr) for c in range(0, kn.shape[-1], LANES)],
        axis=-1).astype(BF16)
    vm_ref[...] = jnp.concatenate(
        [jnp.where(low, one_hot0, kv[:, c:c + LANES]) for c in range(0, kv.shape[-1], LANES)],
        axis=-1).T.astype(BF16)


def _mix_pre(tl, xs, mods, g_norm, p, *, layer):
    d = xs.shape[-1]
    tm = tl.tm
    n_tiles = tl.ctx_tiles + tl.lat_tiles
    t = tl.n_rows
    hs = N_HEADS * HEAD_SLOT

    def const2(shape):
        return pl.BlockSpec(shape, lambda i: (0, 0))

    def rope_spec():
        return pl.BlockSpec((3, tm, LANES), lambda i: (0, tl.rope_block(i), 0))

    def out(width):
        return pl.BlockSpec((tm, width), lambda i: (i, 0))

    def out_t(width):
        return pl.BlockSpec((width, tm), lambda i: (0, i))

    w_in = p["w_in"]
    return pl.pallas_call(
        _mix_pre_kernel,
        out_shape=(
            jax.ShapeDtypeStruct((t, hs), BF16),
            jax.ShapeDtypeStruct((t, LANES), BF16),
            jax.ShapeDtypeStruct((2 * LANES, t), BF16),
            jax.ShapeDtypeStruct((t, CONV_DIM), F32),
            jax.ShapeDtypeStruct((t, CONV_DIM), F32),
            jax.ShapeDtypeStruct((t, hs), BF16),
            jax.ShapeDtypeStruct((t, hs), BF16),
            jax.ShapeDtypeStruct((hs, t), BF16),
        ),
        grid=(n_tiles,),
        in_specs=[
            pl.BlockSpec((tm, d), lambda i: (i, 0)),
            _layer_vec_spec(layer * 3 + 1, d),
            _mod_spec(tl, layer, 0, d),
            _resident(w_in.shape[1:], layer),
            _layer_vec_spec(layer, GQA_Q + GQA_KV),
            _layer_vec_spec(layer, MLA_Q_LORA),
            _layer_vec_spec(layer, MLA_KV_LORA),
            _resident(p["w_uq"].shape[1:], layer),
            _resident(p["w_ukv"].shape[1:], layer),
            _layer_vec_spec(layer, hs),
            _layer_vec_spec(layer, hs),
            _layer_vec_spec(layer, LANES),
            const2((MXU_DIM, MXU_DIM)),
            const2((MXU_DIM, MXU_DIM)),
            const2((MXU_DIM, MXU_DIM)),
            rope_spec(),
            rope_spec(),
        ],
        out_specs=(out(hs), out(LANES), out_t(2 * LANES), out(CONV_DIM), out(CONV_DIM),
                   out(hs), out(hs), out_t(hs)),
        compiler_params=pltpu.CompilerParams(
            dimension_semantics=("arbitrary",), vmem_limit_bytes=VMEM_LIMIT),
        name="mix_pre",
    )(xs, g_norm, mods, w_in, p["g_qk"], p["g_cq"], p["g_ckv"], p["w_uq"], p["w_ukv"],
      p["g_qm"], p["g_km"], p["g_kr"], p["bd_a"], p["bd_q"], p["bd_k"], p["tab_a"], p["tab_m"])


def _attn_kernel(*refs, n_pieces, k_slot, v_slot):
    q_ref = refs[0]
    k_refs = refs[1:1 + 2 * n_pieces:2]
    vt_refs = refs[2:2 + 2 * n_pieces:2]
    o_ref = refs[1 + 2 * n_pieces]

    def scores(h):
        q = q_ref[:, h * HEAD_SLOT:(h + 1) * HEAD_SLOT]
        ks = k_slot[h] * HEAD_SLOT
        return [_dot_nt(k_ref[:, ks:ks + HEAD_SLOT], q) for k_ref in k_refs]

    def weighted_values(h, st):
        vs = v_slot[h] * HEAD_SLOT
        m = jnp.max(st[0], axis=0, keepdims=True)
        for sp in st[1:]:
            m = jnp.maximum(m, jnp.max(sp, axis=0, keepdims=True))
        acc = None
        for sp, vt_ref in zip(st, vt_refs):
            pv = _dot(vt_ref[vs:vs + HEAD_SLOT, :], jnp.exp2(sp - m).astype(BF16))
            acc = pv if acc is None else acc + pv
        return acc[MLA_V:, :] / acc[0:1, :]

    outs = []
    st = scores(0)
    for h in range(N_HEADS):
        st_next = scores(h + 1) if h + 1 < N_HEADS else None
        outs.append(weighted_values(h, st))
        st = st_next
    o_ref[...] = jnp.concatenate(outs, axis=0).T.astype(BF16)


def _attention(tl, q, k, v, *, k_slot, v_slot, latent_queries):
    b, ctx_len, seq = tl.batch, tl.ctx_len, tl.seq
    kw, vw = k.shape[-1], v.shape[0]
    lat_block0 = tl.n_ctx_rows // seq
    if latent_queries:
        tq = min(Q_TILE, seq)
        nq = seq // tq
        q_block0 = tl.n_ctx_rows // tq
        n_rows = b * seq
        pieces = [
            pl.BlockSpec((ctx_len, kw), lambda bi, qi: (bi, 0)),
            pl.BlockSpec((vw, ctx_len), lambda bi, qi: (0, bi)),
            pl.BlockSpec((seq, kw), lambda bi, qi: (lat_block0 + bi, 0)),
            pl.BlockSpec((vw, seq), lambda bi, qi: (0, lat_block0 + bi)),
        ]
        operands = (q, k, v, k, v)
    else:
        tq, nq, q_block0 = ctx_len, 1, 0
        n_rows = b * ctx_len
        pieces = [
            pl.BlockSpec((ctx_len, kw), lambda bi, qi: (bi, 0)),
            pl.BlockSpec((vw, ctx_len), lambda bi, qi: (0, bi)),
        ]
        operands = (q, k, v)
    ow = N_HEADS * MLA_V
    return pl.pallas_call(
        functools.partial(_attn_kernel, n_pieces=len(pieces) // 2, k_slot=k_slot, v_slot=v_slot),
        out_shape=jax.ShapeDtypeStruct((n_rows, ow), BF16),
        grid=(b, nq),
        in_specs=[pl.BlockSpec((tq, N_HEADS * HEAD_SLOT), lambda bi, qi: (q_block0 + bi * nq + qi, 0))]
        + pieces,
        out_specs=pl.BlockSpec((tq, ow), lambda bi, qi: (bi * nq + qi, 0)),
        compiler_params=pltpu.CompilerParams(
            dimension_semantics=("arbitrary", "arbitrary"), vmem_limit_bytes=VMEM_LIMIT),
        name="attention",
    )(*operands)


def _mod_const(v, n):
    return v & (n - 1) if n & (n - 1) == 0 else lax.rem(v, n)

def _mix_post_kernel(*refs, tl, first_tile, n_y):
    ya_refs, ym_refs = refs[:n_y], refs[n_y:2 * n_y]
    x_ref, mod_ref, u_ref, up_ref, un_ref, bg_ref, cw_ref, cb_ref, w_out_ref, o_ref = refs[2 * n_y:]
    tm = x_ref.shape[0]
    i = pl.program_id(0) + first_tile
    is_ctx = i < tl.ctx_tiles
    seq_len = jnp.where(is_ctx, tl.ctx_len, tl.seq)
    local = lax.broadcasted_iota(jnp.int32, (tm, CONV_DIM), 0)
    row = local + i * tm
    pos = jnp.where(is_ctx, _mod_const(row, tl.ctx_len), _mod_const(row, tl.seq))
    u = u_ref[...]
    prev = jnp.where(local == 0, up_ref[SUBLANES - 1:SUBLANES, :], pltpu.roll(u, 1, axis=0))
    prev = jnp.where(pos == 0, 0.0, prev)
    nxt = jnp.where(local == tm - 1, un_ref[0:1, :], pltpu.roll(u, tm - 1, axis=0))
    nxt = jnp.where(pos == seq_len - 1, 0.0, nxt)
    y = prev * cw_ref[0:1, :] + u * cw_ref[1:2, :] + nxt * cw_ref[2:3, :] + cb_ref[...]
    yc = (bg_ref[...] * y).astype(BF16)
    mixed = jnp.concatenate([_rows_load(tl, ya_refs), yc, _rows_load(tl, ym_refs)], axis=-1)
    o_ref[...] = x_ref[...] + mod_ref[5:6, :] * _dot(mixed, w_out_ref[...])


def _mix_post(tl, xs, mods, ya, ym, u, bg, p, *, layer, first_tile, n_tiles):
    d = xs.shape[-1]
    tm = tl.tm
    rows8 = tm // SUBLANES
    last8 = tl.n_rows // SUBLANES - 1

    def tile(width):
        return pl.BlockSpec((tm, width), lambda i: (i + first_tile, 0))

    ya_args, ym_args = _rows_args(ya), _rows_args(ym)
    return pl.pallas_call(
        functools.partial(_mix_post_kernel, tl=tl, first_tile=first_tile, n_y=len(ya_args)),
        out_shape=jax.ShapeDtypeStruct((n_tiles * tm, d), F32),
        grid=(n_tiles,),
        in_specs=_rows_specs(tl, ya, GQA_Q, first_tile) + _rows_specs(tl, ym, N_HEADS * MLA_V, first_tile) + [
            tile(d),
            _mod_spec(tl, layer, first_tile, d),
            tile(CONV_DIM),
            pl.BlockSpec((SUBLANES, CONV_DIM),
                         lambda i: (jnp.maximum((i + first_tile) * rows8 - 1, 0), 0)),
            pl.BlockSpec((SUBLANES, CONV_DIM),
                         lambda i: (jnp.minimum((i + first_tile + 1) * rows8, last8), 0)),
            tile(CONV_DIM),
            pl.BlockSpec((None, CONV_WIDTH, CONV_DIM), lambda i: (layer, 0, 0)),
            _layer_vec_spec(layer, CONV_DIM),
            _resident(p["w_out"].shape[1:], layer),
        ],
        out_specs=pl.BlockSpec((tm, d), lambda i: (i, 0)),
        compiler_params=pltpu.CompilerParams(
            dimension_semantics=("arbitrary",), vmem_limit_bytes=VMEM_LIMIT),
        name="mix_post",
    )(*ya_args, *ym_args, xs, mods, u, u, u, bg, p["conv_w"], p["conv_b"], p["w_out"])


def _block_diag_mean(pattern):
    m = jnp.zeros((MXU_DIM, MXU_DIM), F32)
    for base in range(0, MXU_DIM, LANES):
        o = base
        for width, is_group in pattern:
            if is_group:
                m = m.at[o:o + width, o:o + width].set(1.0 / width)
            o += width
    return m.astype(BF16)


def _rope_tables(seq, tm, head_dim, lane_offset):
    rows = seq // GRID_W
    row = jnp.repeat(jnp.arange(rows, dtype=F32), GRID_W)
    col = jnp.tile(jnp.arange(GRID_W, dtype=F32), rows)
    half = head_dim // 2
    inv = 1.0 / (ROPE_THETA ** (jnp.arange(0, half, 2, dtype=F32) / half))
    ar = row[:, None] * inv[None, :]
    ac = col[:, None] * inv[None, :]
    ang = jnp.concatenate([ar, ar, ac, ac], axis=-1)
    cos, sin = jnp.cos(ang), jnp.sin(ang)
    quarter = head_dim // 4
    up_mask = (jnp.arange(head_dim) // quarter) % 2 == 0
    sin_up = jnp.where(up_mask, -sin, 0.0)
    sin_dn = jnp.where(up_mask, 0.0, sin)
    reps = (LANES - lane_offset) // head_dim if lane_offset == 0 else 1

    def place(t, fill):
        t = jnp.tile(t, (1, reps))
        left = jnp.full((seq, lane_offset), fill, F32)
        right = jnp.zeros((seq, LANES - lane_offset - t.shape[1]), F32)
        return jnp.concatenate([left, t, right], axis=-1)

    lat = jnp.stack([place(cos, 1.0), place(sin_up, 0.0), place(sin_dn, 0.0)])
    ident = jnp.stack([jnp.ones((tm, LANES), F32), jnp.zeros((tm, LANES), F32), jnp.zeros((tm, LANES), F32)])
    if lane_offset:
        keep = (jnp.arange(LANES) < lane_offset + head_dim).astype(F32)
        ident = ident * keep
    return jnp.concatenate([ident, lat], axis=1)


def _prepare(tl, w_in, w_out, gqa_g_q, gqa_g_k, conv_w, conv_b, mla_g_cq, mla_g_ckv, mla_w_uq,
             mla_w_ukv, mla_g_qn, mla_g_kn, mla_g_qr, mla_g_kr):
    depth, d, _ = w_in.shape
    pad = MLA_IN_PAD - MLA_IN
    w_in_p = jnp.concatenate([w_in, jnp.zeros((depth, d, pad), w_in.dtype)], axis=-1).astype(BF16)
    dq = MLA_NOPE + MLA_ROPE
    wq = mla_w_uq.reshape(depth, MLA_Q_LORA, MLA_HEADS, dq)
    wq = jnp.concatenate([wq, jnp.zeros((depth, MLA_Q_LORA, MLA_HEADS, HEAD_SLOT - dq), wq.dtype)], axis=-1)
    w_uq = wq.reshape(depth, MLA_Q_LORA, MLA_HEADS * HEAD_SLOT).astype(BF16)

    def vec(a):
        return a.reshape(depth, 1, a.shape[-1]).astype(F32)

    zeros32 = jnp.zeros((depth, HEAD_SLOT - dq), F32)
    mla_scale = dq ** -0.5 * LOG2_E
    g_qm = jnp.tile(jnp.concatenate([mla_g_qn * mla_scale, mla_g_qr * mla_scale, zeros32], axis=-1),
                    (1, MLA_HEADS))
    g_km = jnp.tile(jnp.concatenate([mla_g_kn, jnp.zeros((depth, MLA_V), F32)], axis=-1), (1, MLA_HEADS))
    g_kr = jnp.concatenate([mla_g_kr, jnp.zeros((depth, LANES - MLA_ROPE), F32)], axis=-1)
    gqa_scale = GQA_HEAD_DIM ** -0.5 * LOG2_E
    g_qk = jnp.concatenate([jnp.tile(gqa_g_q * gqa_scale, (1, GQA_HEADS)),
                            jnp.tile(gqa_g_k, (1, GQA_KV_HEADS))], axis=-1)
    return {
        "w_in": w_in_p,
        "w_out": w_out.astype(BF16),
        "w_uq": w_uq,
        "w_ukv": mla_w_ukv.astype(BF16),
        "g_qk": vec(g_qk), "g_cq": vec(mla_g_cq), "g_ckv": vec(mla_g_ckv),
        "g_qm": vec(g_qm), "g_km": vec(g_km), "g_kr": vec(g_kr),
        "conv_w": conv_w.astype(F32), "conv_b": vec(conv_b),
        "bd_a": _block_diag_mean([(64, True), (64, True)]),
        "bd_q": _block_diag_mean([(MLA_NOPE, True), (MLA_ROPE, True)]),
        "bd_k": _block_diag_mean([(MLA_NOPE, True)]),
        "tab_a": _rope_tables(tl.seq, tl.tm, GQA_HEAD_DIM, 0),
        "tab_m": _rope_tables(tl.seq, tl.tm, MLA_ROPE, MLA_NOPE),
    }


GQA_K_SLOT = (0,) * N_HEADS
GQA_V_SLOT = tuple(h // (GQA_HEADS // GQA_KV_HEADS) for h in range(N_HEADS))
MLA_SLOT = tuple(range(N_HEADS))


def kernel(x, c, ctx, c_ctx, w_mod, b_mod, g_norm, ffn_w_gate, ffn_w_up, ffn_w_down, w_in, w_out,
           gqa_g_q, gqa_g_k, conv_w, conv_b, mla_g_cq, mla_g_ckv, mla_w_uq, mla_w_ukv, mla_g_qn,
           mla_g_kn, mla_g_qr, mla_g_kr):
    batch, seq, d = x.shape
    ctx_len = ctx.shape[1]
    depth = w_mod.shape[0]
    d_ff = ffn_w_gate.shape[-1]
    assert d == 1024 and batch + 1 <= MOD_ROWS and seq % GRID_W == 0
    tl = _Tiling(batch, ctx_len, seq)
    n_all = tl.ctx_tiles + tl.lat_tiles

    cvec = jnp.concatenate([c_ctx[None, :], c, jnp.zeros((MOD_ROWS - 1 - batch, d), F32)], axis=0)
    mods = _modulation(cvec, w_mod, b_mod).reshape(depth, MOD_ROWS, N_MOD, d)
    p = _prepare(tl, w_in, w_out, gqa_g_q, gqa_g_k, conv_w, conv_b, mla_g_cq, mla_g_ckv, mla_w_uq,
                 mla_w_ukv, mla_g_qn, mla_g_kn, mla_g_qr, mla_g_kr)
    g3 = g_norm.reshape(depth * 3, 1, d)
    wg = ffn_w_gate.reshape(depth * 2, d, d_ff).astype(BF16)
    wu = ffn_w_up.reshape(depth * 2, d, d_ff).astype(BF16)
    wd = ffn_w_down.reshape(depth * 2, d_ff, d).astype(BF16)

    xs = (ctx.reshape(batch * ctx_len, d), x.reshape(batch * seq, d))
    for layer in range(depth):
        need_ctx = layer < depth - 1
        xs = _ffn(tl, xs, mods, g3, wg, wu, wd, layer=layer, which=0, first_tile=0, n_tiles=n_all)
        qa, ka, va, u, bg, qm, km, vm = _mix_pre(tl, xs, mods, g3, p, layer=layer)
        ya = _attention(tl, qa, ka, va, k_slot=GQA_K_SLOT, v_slot=GQA_V_SLOT, latent_queries=True)
        ym = _attention(tl, qm, km, vm, k_slot=MLA_SLOT, v_slot=MLA_SLOT, latent_queries=True)
        if need_ctx:
            ya_c = _attention(tl, qa, ka, va, k_slot=GQA_K_SLOT, v_slot=GQA_V_SLOT, latent_queries=False)
            ym_c = _attention(tl, qm, km, vm, k_slot=MLA_SLOT, v_slot=MLA_SLOT, latent_queries=False)
            ya, ym = (ya_c, ya), (ym_c, ym)
            first, n_tiles = 0, n_all
        else:
            first, n_tiles = tl.ctx_tiles, tl.lat_tiles
        xs = _mix_post(tl, xs, mods, ya, ym, u, bg, p, layer=layer, first_tile=first, n_tiles=n_tiles)
        xs = _ffn(tl, xs, mods, g3, wg, wu, wd, layer=layer, which=1, first_tile=first, n_tiles=n_tiles)
    return xs.reshape(batch, seq, d)
```

```python
import functools

import jax
import jax.numpy as jnp
from jax import lax
from jax.experimental import pallas as pl
from jax.experimental.pallas import tpu as pltpu

F32 = jnp.float32
BF16 = jnp.bfloat16

GRID_W = 64
N_MOD = 9
EPS = 1e-6
ROPE_THETA = 10000.0
LOG2_E = 1.4426950408889634
GQA_HEADS = 6
GQA_KV_HEADS = 2
GQA_HEAD_DIM = 64
GQA_Q = GQA_HEADS * GQA_HEAD_DIM
GQA_KV = GQA_KV_HEADS * GQA_HEAD_DIM
GQA_IN = GQA_Q + 2 * GQA_KV
CONV_DIM = 256
CONV_WIDTH = 3
CONV_IN = 3 * CONV_DIM
MLA_HEADS = 6
MLA_Q_LORA = 384
MLA_KV_LORA = 256
MLA_NOPE = 64
MLA_ROPE = 32
MLA_V = 64
MLA_IN = MLA_Q_LORA + MLA_KV_LORA + MLA_ROPE

LANES = 128
SUBLANES = 8
MXU_DIM = 256
HEAD_SLOT = LANES
N_HEADS = 6
MLA_IN_PAD = 768
MOD_ROWS = 16
VMEM_LIMIT = 56 * 1024 * 1024
TOKEN_TILE = 512
FFN_TILE = 1024
Q_TILE = 1024
FF_CHUNK = 512


def _dot(a, b):
    return jnp.dot(a, b, preferred_element_type=F32)


def _dot_nt(a, b):
    return lax.dot_general(a, b, (((1,), (1,)), ((), ())), preferred_element_type=F32)


def _silu(x):
    return x * jax.nn.sigmoid(x)


def _row_rms_scale(x, width):
    return lax.rsqrt(jnp.sum(x * x, axis=-1, keepdims=True) * (1.0 / width) + EPS)


def _norm_mod(x, g, shift, scale):
    y = x * _row_rms_scale(x, x.shape[-1])
    return (y * g) * (1.0 + scale) + shift


def _group_mean_sq(x, bd_ref):
    cols = []
    bd = bd_ref[...]
    for c in range(0, x.shape[-1], MXU_DIM):
        sq = x[:, c:c + MXU_DIM]
        cols.append(_dot((sq * sq).astype(BF16), bd))
    return jnp.concatenate(cols, axis=-1)


def _rope_cols(x, tab_ref, quarter):
    cos, sin_up, sin_dn = tab_ref[0], tab_ref[1], tab_ref[2]
    cols = []
    for c in range(0, x.shape[-1], LANES):
        v = x[:, c:c + LANES]
        up = pltpu.roll(v, LANES - quarter, axis=1)
        dn = pltpu.roll(v, quarter, axis=1)
        cols.append(v * cos + up * sin_up + dn * sin_dn)
    return cols


def _mod_kernel(c_ref, w_ref, b_ref, o_ref):
    s = _silu(c_ref[...]).astype(BF16)
    o_ref[...] = _dot(s, w_ref[...].astype(BF16)) + b_ref[...]


def _modulation(cvec, w_mod, b_mod):
    depth, d, n = w_mod.shape
    tn = 1024
    return pl.pallas_call(
        _mod_kernel,
        out_shape=jax.ShapeDtypeStruct((depth, MOD_ROWS, n), F32),
        grid=(depth, n // tn),
        in_specs=[
            pl.BlockSpec((MOD_ROWS, d), lambda l, j: (0, 0)),
            pl.BlockSpec((None, d, tn), lambda l, j: (l, 0, j)),
            pl.BlockSpec((None, 1, tn), lambda l, j: (l, 0, j)),
        ],
        out_specs=pl.BlockSpec((None, MOD_ROWS, tn), lambda l, j: (l, 0, j)),
        compiler_params=pltpu.CompilerParams(
            dimension_semantics=("arbitrary", "arbitrary"), vmem_limit_bytes=VMEM_LIMIT),
        name="modulation",
    )(cvec, w_mod, b_mod.reshape(depth, 1, n))


class _Tiling:
    def __init__(self, batch, ctx_len, seq, tm):
        self.batch, self.ctx_len, self.seq = batch, ctx_len, seq
        self.n_ctx_rows = batch * ctx_len
        self.n_rows = self.n_ctx_rows + batch * seq
        while self.n_ctx_rows % tm or seq % tm:
            tm //= 2
        assert tm >= SUBLANES
        self.tm = tm
        self.ctx_tiles = self.n_ctx_rows // tm
        self.lat_tiles = batch * seq // tm
        self.tiles_per_seq = seq // tm
        assert self.n_ctx_rows % seq == 0

    def mod_row(self, i):
        return jnp.where(i < self.ctx_tiles, 0, 1 + (i - self.ctx_tiles) // self.tiles_per_seq)

    def rope_block(self, i):
        return jnp.where(i < self.ctx_tiles, 0, 1 + (i - self.ctx_tiles) % self.tiles_per_seq)


def _mod_spec(tl, layer, first_tile, d):
    return pl.BlockSpec((None, None, N_MOD, d), lambda i: (layer, tl.mod_row(i + first_tile), 0, 0))


def _layer_vec_spec(layer, width):
    return pl.BlockSpec((None, 1, width), lambda i: (layer, 0, 0))


def _resident(shape, layer):
    nd = len(shape)
    return pl.BlockSpec((None,) + tuple(shape), lambda i: (layer,) + (0,) * nd,
                        pipeline_mode=pl.Buffered(1))


def _ffn_body(x, g, mod_ref, base, wg_ref, wu_ref, wd_ref):
    h = _norm_mod(x, g, mod_ref[base:base + 1, :], mod_ref[base + 1:base + 2, :]).astype(BF16)
    d_ff = wg_ref.shape[-1]
    acc = jnp.zeros(x.shape, F32)
    for c0 in range(0, d_ff, FF_CHUNK):
        c1 = min(c0 + FF_CHUNK, d_ff)
        gate = _dot(h, wg_ref[:, c0:c1])
        up = _dot(h, wu_ref[:, c0:c1])
        act = (_silu(gate) * up).astype(BF16)
        acc = acc + _dot(act, wd_ref[c0:c1, :])
    return x + (0.5 * mod_ref[base + 2:base + 3, :]) * acc


def _rows_specs(tl, rows, width, first_tile):
    tm = tl.tm
    if not isinstance(rows, tuple):
        return [pl.BlockSpec((tm, width), lambda i: (i, 0))]
    assert first_tile == 0
    return [pl.BlockSpec((tm, width), lambda i: (jnp.minimum(i, tl.ctx_tiles - 1), 0)),
            pl.BlockSpec((tm, width), lambda i: (jnp.maximum(i - tl.ctx_tiles, 0), 0))]


def _rows_load(tl, refs):
    if len(refs) == 1:
        return refs[0][...]
    return jnp.where(pl.program_id(0) < tl.ctx_tiles, refs[0][...], refs[1][...])


def _rows_args(rows):
    return rows if isinstance(rows, tuple) else (rows,)


def _ffn_kernel(*refs, base, tl, n_x):
    g_ref, mod_ref, wg_ref, wu_ref, wd_ref, o_ref = refs[n_x:]
    o_ref[...] = _ffn_body(_rows_load(tl, refs[:n_x]), g_ref[...], mod_ref, base, wg_ref, wu_ref, wd_ref)


def _ffn(tl, xs, mods, g_norm, wg, wu, wd, *, layer, which, first_tile, n_tiles):
    d, d_ff = wg.shape[-2:]
    base = 0 if which == 0 else 6
    gi = layer * 3 + (0 if which == 0 else 2)
    wi = layer * 2 + which
    x_args = _rows_args(xs)
    return pl.pallas_call(
        functools.partial(_ffn_kernel, base=base, tl=tl, n_x=len(x_args)),
        out_shape=jax.ShapeDtypeStruct((n_tiles * tl.tm, d), F32),
        grid=(n_tiles,),
        in_specs=_rows_specs(tl, xs, d, first_tile) + [
            _layer_vec_spec(gi, d),
            _mod_spec(tl, layer, first_tile, d),
            _resident((d, d_ff), wi),
            _resident((d, d_ff), wi),
            _resident((d_ff, d), wi),
        ],
        out_specs=pl.BlockSpec((tl.tm, d), lambda i: (i, 0)),
        compiler_params=pltpu.CompilerParams(
            dimension_semantics=("arbitrary",), vmem_limit_bytes=VMEM_LIMIT),
        name="ffn",
    )(*x_args, g_norm, mods, wg, wu, wd)


def _mix_pre_kernel(x_ref, g_ref, mod_ref, w_in_ref, gqk_ref, gcq_ref, gckv_ref, wuq_ref, wukv_ref,
                    gqm_ref, gkm_ref, gkr_ref, bd_a_ref, bd_q_ref, bd_k_ref, tab_a_ref, tab_m_ref,
                    qa_ref, ka_ref, va_ref, u_ref, bg_ref, qm_ref, km_ref, vm_ref):
    x = x_ref[...]
    h = _norm_mod(x, g_ref[...], mod_ref[3:4, :], mod_ref[4:5, :]).astype(BF16)
    lane = lax.broadcasted_iota(jnp.int32, (x.shape[0], LANES), 1)
    low = lane < GQA_HEAD_DIM
    one_hot0 = (lane == 0).astype(F32)

    pa = _dot(h, w_in_ref[:, 0:GQA_IN])
    qk = pa[:, 0:GQA_Q + GQA_KV]
    qk = qk * lax.rsqrt(_group_mean_sq(qk, bd_a_ref) + EPS) * gqk_ref[...]
    c0, c1, c2, kcol = _rope_cols(qk, tab_a_ref, GQA_HEAD_DIM // 4)
    zero = jnp.zeros_like(c0)
    slots = [
        jnp.where(low, c0, zero),
        jnp.where(low, pltpu.roll(c0, GQA_HEAD_DIM, axis=1), zero),
        jnp.where(low, c1, zero),
        jnp.where(low, zero, c1),
        jnp.where(low, zero, pltpu.roll(c2, GQA_HEAD_DIM, axis=1)),
        jnp.where(low, zero, c2),
    ]
    qa_ref[...] = jnp.concatenate(slots, axis=-1).astype(BF16)
    ka_ref[...] = kcol.astype(BF16)
    vcol = pa[:, GQA_Q + GQA_KV:GQA_IN]
    va_ref[...] = jnp.concatenate(
        [jnp.where(low, one_hot0, pltpu.roll(vcol, GQA_HEAD_DIM, axis=1)),
         jnp.where(low, one_hot0, vcol)], axis=-1).T.astype(BF16)

    ps = _dot(h, w_in_ref[:, GQA_IN:GQA_IN + CONV_IN])
    u_ref[...] = ps[:, 2 * CONV_DIM:3 * CONV_DIM] * ps[:, 0:CONV_DIM]
    bg_ref[...] = ps[:, CONV_DIM:2 * CONV_DIM]

    pm = _dot(h, w_in_ref[:, GQA_IN + CONV_IN:GQA_IN + CONV_IN + MLA_IN_PAD])
    cq = pm[:, 0:MLA_Q_LORA]
    cq = (cq * _row_rms_scale(cq, MLA_Q_LORA) * gcq_ref[...]).astype(BF16)
    ckv = pm[:, MLA_Q_LORA:MLA_Q_LORA + MLA_KV_LORA]
    ckv = (ckv * _row_rms_scale(ckv, MLA_KV_LORA) * gckv_ref[...]).astype(BF16)
    q = _dot(cq, wuq_ref[...])
    q = q * lax.rsqrt(_group_mean_sq(q, bd_q_ref) + EPS) * gqm_ref[...]
    qm_ref[...] = jnp.concatenate(_rope_cols(q, tab_m_ref, MLA_ROPE // 4), axis=-1).astype(BF16)
    kr = pm[:, MLA_Q_LORA + MLA_KV_LORA:MLA_IN_PAD]
    kr = kr * _row_rms_scale(kr, MLA_ROPE) * gkr_ref[...]
    kr = _rope_cols(pltpu.roll(kr, MLA_NOPE, axis=1), tab_m_ref, MLA_ROPE // 4)[0]
    kv = _dot(ckv, wukv_ref[...])
    kn = kv * lax.rsqrt(_group_mean_sq(kv, bd_k_ref) + EPS) * gkm_ref[...]
    km_ref[...] = jnp.concatenate(
        [jnp.where(low, kn[:, c:c + LANES], kr) for c in range(0, kn.shape[-1], LANES)],
        axis=-1).astype(BF16)
    vm_ref[...] = jnp.concatenate(
        [jnp.where(low, one_hot0, kv[:, c:c + LANES]) for c in range(0, kv.shape[-1], LANES)],
        axis=-1).T.astype(BF16)


def _mix_pre(tl, xs, mods, g_norm, p, *, layer):
    d = xs.shape[-1]
    tm = tl.tm
    n_tiles = tl.ctx_tiles + tl.lat_tiles
    t = tl.n_rows
    hs = N_HEADS * HEAD_SLOT

    def const2(shape):
        return pl.BlockSpec(shape, lambda i: (0, 0))

    def rope_spec():
        return pl.BlockSpec((3, tm, LANES), lambda i: (0, tl.rope_block(i), 0))

    def out(width):
        return pl.BlockSpec((tm, width), lambda i: (i, 0))

    def out_t(width):
        return pl.BlockSpec((width, tm), lambda i: (0, i))

    w_in = p["w_in"]
    return pl.pallas_call(
        _mix_pre_kernel,
        out_shape=(
            jax.ShapeDtypeStruct((t, hs), BF16),
            jax.ShapeDtypeStruct((t, LANES), BF16),
            jax.ShapeDtypeStruct((2 * LANES, t), BF16),
            jax.ShapeDtypeStruct((t, CONV_DIM), F32),
            jax.ShapeDtypeStruct((t, CONV_DIM), F32),
            jax.ShapeDtypeStruct((t, hs), BF16),
            jax.ShapeDtypeStruct((t, hs), BF16),
            jax.ShapeDtypeStruct((hs, t), BF16),
        ),
        grid=(n_tiles,),
        in_specs=[
            pl.BlockSpec((tm, d), lambda i: (i, 0)),
            _layer_vec_spec(layer * 3 + 1, d),
            _mod_spec(tl, layer, 0, d),
            _resident(w_in.shape[1:], layer),
            _layer_vec_spec(layer, GQA_Q + GQA_KV),
            _layer_vec_spec(layer, MLA_Q_LORA),
            _layer_vec_spec(layer, MLA_KV_LORA),
            _resident(p["w_uq"].shape[1:], layer),
            _resident(p["w_ukv"].shape[1:], layer),
            _layer_vec_spec(layer, hs),
            _layer_vec_spec(layer, hs),
            _layer_vec_spec(layer, LANES),
            const2((MXU_DIM, MXU_DIM)),
            const2((MXU_DIM, MXU_DIM)),
            const2((MXU_DIM, MXU_DIM)),
            rope_spec(),
            rope_spec(),
        ],
        out_specs=(out(hs), out(LANES), out_t(2 * LANES), out(CONV_DIM), out(CONV_DIM),
                   out(hs), out(hs), out_t(hs)),
        compiler_params=pltpu.CompilerParams(
            dimension_semantics=("arbitrary",), vmem_limit_bytes=VMEM_LIMIT),
        name="mix_pre",
    )(xs, g_norm, mods, w_in, p["g_qk"], p["g_cq"], p["g_ckv"], p["w_uq"], p["w_ukv"],
      p["g_qm"], p["g_km"], p["g_kr"], p["bd_a"], p["bd_q"], p["bd_k"], p["tab_a"], p["tab_m"])


def _attend(q_ref, k_refs, vt_refs, o_ref, k_slot, v_slot):
    def scores(h):
        q = q_ref[:, h * HEAD_SLOT:(h + 1) * HEAD_SLOT]
        ks = k_slot[h] * HEAD_SLOT
        return [_dot_nt(k_ref[:, ks:ks + HEAD_SLOT], q) for k_ref in k_refs]

    def weighted_values(h, st):
        vs = v_slot[h] * HEAD_SLOT
        m = jnp.max(st[0], axis=0, keepdims=True)
        for sp in st[1:]:
            m = jnp.maximum(m, jnp.max(sp, axis=0, keepdims=True))
        acc = None
        for sp, vt_ref in zip(st, vt_refs):
            pv = _dot(vt_ref[vs:vs + HEAD_SLOT, :], jnp.exp2(sp - m).astype(BF16))
            acc = pv if acc is None else acc + pv
        return acc[MLA_V:, :] / acc[0:1, :]

    outs = []
    st = scores(0)
    for h in range(N_HEADS):
        st_next = scores(h + 1) if h + 1 < N_HEADS else None
        outs.append(weighted_values(h, st))
        st = st_next
    o_ref[...] = jnp.concatenate(outs, axis=0).T.astype(BF16)


def _attn_kernel(*refs, nq, ctx_queries, k_slot, v_slot):
    if ctx_queries:
        q_ref, qc_ref, kc_ref, vtc_ref, kl_ref, vtl_ref, o_ref, oc_ref = refs
    else:
        q_ref, kc_ref, vtc_ref, kl_ref, vtl_ref, o_ref = refs
    step = pl.program_id(1)

    @pl.when(step < nq)
    def _():
        _attend(q_ref, (kc_ref, kl_ref), (vtc_ref, vtl_ref), o_ref, k_slot, v_slot)

    if ctx_queries:
        @pl.when(step == nq)
        def _():
            _attend(qc_ref, (kc_ref,), (vtc_ref,), oc_ref, k_slot, v_slot)


def _attention(tl, q, k, v, *, k_slot, v_slot, ctx_queries):
    b, ctx_len, seq = tl.batch, tl.ctx_len, tl.seq
    kw, vw = k.shape[-1], v.shape[0]
    lat_block0 = tl.n_ctx_rows // seq
    tq = min(Q_TILE, seq)
    nq = seq // tq
    q_block0 = tl.n_ctx_rows // tq
    qw, ow = N_HEADS * HEAD_SLOT, N_HEADS * MLA_V

    def lat_tile(bi, qi):
        return bi * nq + jnp.minimum(qi, nq - 1)

    q_specs = [pl.BlockSpec((tq, qw), lambda bi, qi: (q_block0 + lat_tile(bi, qi), 0))]
    out_specs = [pl.BlockSpec((tq, ow), lambda bi, qi: (lat_tile(bi, qi), 0))]
    out_shape = [jax.ShapeDtypeStruct((b * seq, ow), BF16)]
    operands = [q]
    if ctx_queries:
        q_specs.append(pl.BlockSpec((ctx_len, qw), lambda bi, qi: (bi, 0)))
        out_specs.append(pl.BlockSpec((ctx_len, ow), lambda bi, qi: (bi, 0)))
        out_shape.append(jax.ShapeDtypeStruct((b * ctx_len, ow), BF16))
        operands.append(q)
    kv_specs = [
        pl.BlockSpec((ctx_len, kw), lambda bi, qi: (bi, 0)),
        pl.BlockSpec((vw, ctx_len), lambda bi, qi: (0, bi)),
        pl.BlockSpec((seq, kw), lambda bi, qi: (lat_block0 + bi, 0)),
        pl.BlockSpec((vw, seq), lambda bi, qi: (0, lat_block0 + bi)),
    ]
    outs = pl.pallas_call(
        functools.partial(_attn_kernel, nq=nq, ctx_queries=ctx_queries, k_slot=k_slot, v_slot=v_slot),
        out_shape=out_shape,
        grid=(b, nq + (1 if ctx_queries else 0)),
        in_specs=q_specs + kv_specs,
        out_specs=out_specs,
        compiler_params=pltpu.CompilerParams(
            dimension_semantics=("arbitrary", "arbitrary"), vmem_limit_bytes=VMEM_LIMIT),
        name="attention",
    )(*operands, k, v, k, v)
    return (outs[0], outs[1]) if ctx_queries else (outs[0], None)


def _mod_const(v, n):
    return v & (n - 1) if n & (n - 1) == 0 else lax.rem(v, n)

def _mix_post_kernel(*refs, tl, first_tile, n_y):
    ya_refs, ym_refs = refs[:n_y], refs[n_y:2 * n_y]
    (x_ref, mod_ref, u_ref, up_ref, un_ref, bg_ref, cw_ref, cb_ref, w_out_ref,
     g_ref, wg_ref, wu_ref, wd_ref, o_ref) = refs[2 * n_y:]
    tm = x_ref.shape[0]
    i = pl.program_id(0) + first_tile
    is_ctx = i < tl.ctx_tiles
    seq_len = jnp.where(is_ctx, tl.ctx_len, tl.seq)
    local = lax.broadcasted_iota(jnp.int32, (tm, CONV_DIM), 0)
    row = local + i * tm
    pos = jnp.where(is_ctx, _mod_const(row, tl.ctx_len), _mod_const(row, tl.seq))
    u = u_ref[...]
    prev = jnp.where(local == 0, up_ref[SUBLANES - 1:SUBLANES, :], pltpu.roll(u, 1, axis=0))
    prev = jnp.where(pos == 0, 0.0, prev)
    nxt = jnp.where(local == tm - 1, un_ref[0:1, :], pltpu.roll(u, tm - 1, axis=0))
    nxt = jnp.where(pos == seq_len - 1, 0.0, nxt)
    y = prev * cw_ref[0:1, :] + u * cw_ref[1:2, :] + nxt * cw_ref[2:3, :] + cb_ref[...]
    yc = (bg_ref[...] * y).astype(BF16)
    mixed = jnp.concatenate([_rows_load(tl, ya_refs), yc, _rows_load(tl, ym_refs)], axis=-1)
    x = x_ref[...] + mod_ref[5:6, :] * _dot(mixed, w_out_ref[...])
    o_ref[...] = _ffn_body(x, g_ref[...], mod_ref, 6, wg_ref, wu_ref, wd_ref)


def _mix_post(tl, xs, mods, ya, ym, u, bg, p, g_norm, wg, wu, wd, *, layer, first_tile, n_tiles):
    d = xs.shape[-1]
    d_ff = wg.shape[-1]
    tm = tl.tm
    rows8 = tm // SUBLANES
    last8 = tl.n_rows // SUBLANES - 1

    def tile(width):
        return pl.BlockSpec((tm, width), lambda i: (i + first_tile, 0))

    ya_args, ym_args = _rows_args(ya), _rows_args(ym)
    return pl.pallas_call(
        functools.partial(_mix_post_kernel, tl=tl, first_tile=first_tile, n_y=len(ya_args)),
        out_shape=jax.ShapeDtypeStruct((n_tiles * tm, d), F32),
        grid=(n_tiles,),
        in_specs=_rows_specs(tl, ya, GQA_Q, first_tile) + _rows_specs(tl, ym, N_HEADS * MLA_V, first_tile) + [
            tile(d),
            _mod_spec(tl, layer, first_tile, d),
            tile(CONV_DIM),
            pl.BlockSpec((SUBLANES, CONV_DIM),
                         lambda i: (jnp.maximum((i + first_tile) * rows8 - 1, 0), 0)),
            pl.BlockSpec((SUBLANES, CONV_DIM),
                         lambda i: (jnp.minimum((i + first_tile + 1) * rows8, last8), 0)),
            tile(CONV_DIM),
            pl.BlockSpec((None, CONV_WIDTH, CONV_DIM), lambda i: (layer, 0, 0)),
            _layer_vec_spec(layer, CONV_DIM),
            _resident(p["w_out"].shape[1:], layer),
            _layer_vec_spec(layer * 3 + 2, d),
            _resident((d, d_ff), layer * 2 + 1),
            _resident((d, d_ff), layer * 2 + 1),
            _resident((d_ff, d), layer * 2 + 1),
        ],
        out_specs=pl.BlockSpec((tm, d), lambda i: (i, 0)),
        compiler_params=pltpu.CompilerParams(
            dimension_semantics=("arbitrary",), vmem_limit_bytes=VMEM_LIMIT),
        name="mix_post",
    )(*ya_args, *ym_args, xs, mods, u, u, u, bg, p["conv_w"], p["conv_b"], p["w_out"],
      g_norm, wg, wu, wd)


def _block_diag_mean(pattern):
    m = jnp.zeros((MXU_DIM, MXU_DIM), F32)
    for base in range(0, MXU_DIM, LANES):
        o = base
        for width, is_group in pattern:
            if is_group:
                m = m.at[o:o + width, o:o + width].set(1.0 / width)
            o += width
    return m.astype(BF16)


def _rope_tables(seq, tm, head_dim, lane_offset):
    rows = seq // GRID_W
    row = jnp.repeat(jnp.arange(rows, dtype=F32), GRID_W)
    col = jnp.tile(jnp.arange(GRID_W, dtype=F32), rows)
    half = head_dim // 2
    inv = 1.0 / (ROPE_THETA ** (jnp.arange(0, half, 2, dtype=F32) / half))
    ar = row[:, None] * inv[None, :]
    ac = col[:, None] * inv[None, :]
    ang = jnp.concatenate([ar, ar, ac, ac], axis=-1)
    cos, sin = jnp.cos(ang), jnp.sin(ang)
    quarter = head_dim // 4
    up_mask = (jnp.arange(head_dim) // quarter) % 2 == 0
    sin_up = jnp.where(up_mask, -sin, 0.0)
    sin_dn = jnp.where(up_mask, 0.0, sin)
    reps = (LANES - lane_offset) // head_dim if lane_offset == 0 else 1

    def place(t, fill):
        t = jnp.tile(t, (1, reps))
        left = jnp.full((seq, lane_offset), fill, F32)
        right = jnp.zeros((seq, LANES - lane_offset - t.shape[1]), F32)
        return jnp.concatenate([left, t, right], axis=-1)

    lat = jnp.stack([place(cos, 1.0), place(sin_up, 0.0), place(sin_dn, 0.0)])
    ident = jnp.stack([jnp.ones((tm, LANES), F32), jnp.zeros((tm, LANES), F32), jnp.zeros((tm, LANES), F32)])
    if lane_offset:
        keep = (jnp.arange(LANES) < lane_offset + head_dim).astype(F32)
        ident = ident * keep
    return jnp.concatenate([ident, lat], axis=1)


def _prepare(tl, w_in, w_out, gqa_g_q, gqa_g_k, conv_w, conv_b, mla_g_cq, mla_g_ckv, mla_w_uq,
             mla_w_ukv, mla_g_qn, mla_g_kn, mla_g_qr, mla_g_kr):
    depth, d, _ = w_in.shape
    pad = MLA_IN_PAD - MLA_IN
    w_in_p = jnp.concatenate([w_in, jnp.zeros((depth, d, pad), w_in.dtype)], axis=-1).astype(BF16)
    dq = MLA_NOPE + MLA_ROPE
    wq = mla_w_uq.reshape(depth, MLA_Q_LORA, MLA_HEADS, dq)
    wq = jnp.concatenate([wq, jnp.zeros((depth, MLA_Q_LORA, MLA_HEADS, HEAD_SLOT - dq), wq.dtype)], axis=-1)
    w_uq = wq.reshape(depth, MLA_Q_LORA, MLA_HEADS * HEAD_SLOT).astype(BF16)

    def vec(a):
        return a.reshape(depth, 1, a.shape[-1]).astype(F32)

    zeros32 = jnp.zeros((depth, HEAD_SLOT - dq), F32)
    mla_scale = dq ** -0.5 * LOG2_E
    g_qm = jnp.tile(jnp.concatenate([mla_g_qn * mla_scale, mla_g_qr * mla_scale, zeros32], axis=-1),
                    (1, MLA_HEADS))
    g_km = jnp.tile(jnp.concatenate([mla_g_kn, jnp.zeros((depth, MLA_V), F32)], axis=-1), (1, MLA_HEADS))
    g_kr = jnp.concatenate([mla_g_kr, jnp.zeros((depth, LANES - MLA_ROPE), F32)], axis=-1)
    gqa_scale = GQA_HEAD_DIM ** -0.5 * LOG2_E
    g_qk = jnp.concatenate([jnp.tile(gqa_g_q * gqa_scale, (1, GQA_HEADS)),
                            jnp.tile(gqa_g_k, (1, GQA_KV_HEADS))], axis=-1)
    return {
        "w_in": w_in_p,
        "w_out": w_out.astype(BF16),
        "w_uq": w_uq,
        "w_ukv": mla_w_ukv.astype(BF16),
        "g_qk": vec(g_qk), "g_cq": vec(mla_g_cq), "g_ckv": vec(mla_g_ckv),
        "g_qm": vec(g_qm), "g_km": vec(g_km), "g_kr": vec(g_kr),
        "conv_w": conv_w.astype(F32), "conv_b": vec(conv_b),
        "bd_a": _block_diag_mean([(64, True), (64, True)]),
        "bd_q": _block_diag_mean([(MLA_NOPE, True), (MLA_ROPE, True)]),
        "bd_k": _block_diag_mean([(MLA_NOPE, True)]),
        "tab_a": _rope_tables(tl.seq, tl.tm, GQA_HEAD_DIM, 0),
        "tab_m": _rope_tables(tl.seq, tl.tm, MLA_ROPE, MLA_NOPE),
    }


GQA_K_SLOT = (0,) * N_HEADS
GQA_V_SLOT = tuple(h // (GQA_HEADS // GQA_KV_HEADS) for h in range(N_HEADS))
MLA_SLOT = tuple(range(N_HEADS))


def kernel(x, c, ctx, c_ctx, w_mod, b_mod, g_norm, ffn_w_gate, ffn_w_up, ffn_w_down, w_in, w_out,
           gqa_g_q, gqa_g_k, conv_w, conv_b, mla_g_cq, mla_g_ckv, mla_w_uq, mla_w_ukv, mla_g_qn,
           mla_g_kn, mla_g_qr, mla_g_kr):
    batch, seq, d = x.shape
    ctx_len = ctx.shape[1]
    depth = w_mod.shape[0]
    d_ff = ffn_w_gate.shape[-1]
    assert d == 1024 and batch + 1 <= MOD_ROWS and seq % GRID_W == 0
    tl = _Tiling(batch, ctx_len, seq, TOKEN_TILE)
    tl_ffn = _Tiling(batch, ctx_len, seq, FFN_TILE)
    n_all = tl.ctx_tiles + tl.lat_tiles

    cvec = jnp.concatenate([c_ctx[None, :], c, jnp.zeros((MOD_ROWS - 1 - batch, d), F32)], axis=0)
    mods = _modulation(cvec, w_mod, b_mod).reshape(depth, MOD_ROWS, N_MOD, d)
    p = _prepare(tl, w_in, w_out, gqa_g_q, gqa_g_k, conv_w, conv_b, mla_g_cq, mla_g_ckv, mla_w_uq,
                 mla_w_ukv, mla_g_qn, mla_g_kn, mla_g_qr, mla_g_kr)
    g3 = g_norm.reshape(depth * 3, 1, d)
    wg = ffn_w_gate.reshape(depth * 2, d, d_ff).astype(BF16)
    wu = ffn_w_up.reshape(depth * 2, d, d_ff).astype(BF16)
    wd = ffn_w_down.reshape(depth * 2, d_ff, d).astype(BF16)

    xs = (ctx.reshape(batch * ctx_len, d), x.reshape(batch * seq, d))
    for layer in range(depth):
        need_ctx = layer < depth - 1
        xs = _ffn(tl_ffn, xs, mods, g3, wg, wu, wd, layer=layer, which=0, first_tile=0,
                  n_tiles=tl_ffn.ctx_tiles + tl_ffn.lat_tiles)
        qa, ka, va, u, bg, qm, km, vm = _mix_pre(tl, xs, mods, g3, p, layer=layer)
        ya, ya_c = _attention(tl, qa, ka, va, k_slot=GQA_K_SLOT, v_slot=GQA_V_SLOT, ctx_queries=need_ctx)
        ym, ym_c = _attention(tl, qm, km, vm, k_slot=MLA_SLOT, v_slot=MLA_SLOT, ctx_queries=need_ctx)
        if need_ctx:
            ya, ym = (ya_c, ya), (ym_c, ym)
            first, n_tiles = 0, n_all
        else:
            first, n_tiles = tl.ctx_tiles, tl.lat_tiles
        xs = _mix_post(tl, xs, mods, ya, ym, u, bg, p, g3, wg, wu, wd,
                       layer=layer, first_tile=first, n_tiles=n_tiles)
    return xs.reshape(batch, seq, d)
```

```python
import functools

import jax
import jax.numpy as jnp
from jax import lax
from jax.experimental import pallas as pl
from jax.experimental.pallas import tpu as pltpu

F32 = jnp.float32
BF16 = jnp.bfloat16

GRID_W = 64
N_MOD = 9
EPS = 1e-6
ROPE_THETA = 10000.0
LOG2_E = 1.4426950408889634
GQA_HEADS = 6
GQA_KV_HEADS = 2
GQA_HEAD_DIM = 64
GQA_Q = GQA_HEADS * GQA_HEAD_DIM
GQA_KV = GQA_KV_HEADS * GQA_HEAD_DIM
GQA_IN = GQA_Q + 2 * GQA_KV
CONV_DIM = 256
CONV_WIDTH = 3
CONV_IN = 3 * CONV_DIM
MLA_HEADS = 6
MLA_Q_LORA = 384
MLA_KV_LORA = 256
MLA_NOPE = 64
MLA_ROPE = 32
MLA_V = 64
MLA_IN = MLA_Q_LORA + MLA_KV_LORA + MLA_ROPE

LANES = 128
SUBLANES = 8
MXU_DIM = 256
HEAD_SLOT = LANES
N_HEADS = 6
MLA_IN_PAD = 768
MOD_ROWS = 16
VMEM_LIMIT = 56 * 1024 * 1024
TOKEN_TILE = 512
FFN_TILE = 1024
Q_TILE = 2048
Q_SUB = 1024
FF_CHUNK = 512


def _dot(a, b):
    return jnp.dot(a, b, preferred_element_type=F32)


def _dot_nt(a, b):
    return lax.dot_general(a, b, (((1,), (1,)), ((), ())), preferred_element_type=F32)


def _silu(x):
    return x * jax.nn.sigmoid(x)


def _row_rms_scale(x, width):
    return lax.rsqrt(jnp.sum(x * x, axis=-1, keepdims=True) * (1.0 / width) + EPS)


def _norm_mod(x, g, shift, scale):
    y = x * _row_rms_scale(x, x.shape[-1])
    return (y * g) * (1.0 + scale) + shift


def _group_mean_sq(x, bd_ref):
    cols = []
    bd = bd_ref[...]
    for c in range(0, x.shape[-1], MXU_DIM):
        sq = x[:, c:c + MXU_DIM]
        cols.append(_dot((sq * sq).astype(BF16), bd))
    return jnp.concatenate(cols, axis=-1)


def _rope_cols(x, tab_ref, quarter):
    cos, sin_up, sin_dn = tab_ref[0], tab_ref[1], tab_ref[2]
    cols = []
    for c in range(0, x.shape[-1], LANES):
        v = x[:, c:c + LANES]
        up = pltpu.roll(v, LANES - quarter, axis=1)
        dn = pltpu.roll(v, quarter, axis=1)
        cols.append(v * cos + up * sin_up + dn * sin_dn)
    return cols


def _mod_kernel(c_ref, w_ref, b_ref, o_ref):
    s = _silu(c_ref[...]).astype(BF16)
    o_ref[...] = _dot(s, w_ref[...].astype(BF16)) + b_ref[...]


def _modulation(cvec, w_mod, b_mod):
    depth, d, n = w_mod.shape
    tn = 1024
    return pl.pallas_call(
        _mod_kernel,
        out_shape=jax.ShapeDtypeStruct((depth, MOD_ROWS, n), F32),
        grid=(depth, n // tn),
        in_specs=[
            pl.BlockSpec((MOD_ROWS, d), lambda l, j: (0, 0)),
            pl.BlockSpec((None, d, tn), lambda l, j: (l, 0, j)),
            pl.BlockSpec((None, 1, tn), lambda l, j: (l, 0, j)),
        ],
        out_specs=pl.BlockSpec((None, MOD_ROWS, tn), lambda l, j: (l, 0, j)),
        compiler_params=pltpu.CompilerParams(
            dimension_semantics=("arbitrary", "arbitrary"), vmem_limit_bytes=VMEM_LIMIT),
        name="modulation",
    )(cvec, w_mod, b_mod.reshape(depth, 1, n))


class _Tiling:
    def __init__(self, batch, ctx_len, seq, tm):
        self.batch, self.ctx_len, self.seq = batch, ctx_len, seq
        self.n_ctx_rows = batch * ctx_len
        self.n_rows = self.n_ctx_rows + batch * seq
        while self.n_ctx_rows % tm or seq % tm:
            tm //= 2
        assert tm >= SUBLANES
        self.tm = tm
        self.ctx_tiles = self.n_ctx_rows // tm
        self.lat_tiles = batch * seq // tm
        self.tiles_per_seq = seq // tm
        assert self.n_ctx_rows % seq == 0

    def mod_row(self, i):
        return jnp.where(i < self.ctx_tiles, 0, 1 + (i - self.ctx_tiles) // self.tiles_per_seq)

    def rope_block(self, i):
        return jnp.where(i < self.ctx_tiles, 0, 1 + (i - self.ctx_tiles) % self.tiles_per_seq)


def _mod_spec(tl, layer, first_tile, d):
    return pl.BlockSpec((None, None, N_MOD, d), lambda i: (layer, tl.mod_row(i + first_tile), 0, 0))


def _layer_vec_spec(layer, width):
    return pl.BlockSpec((None, 1, width), lambda i: (layer, 0, 0))


def _resident(shape, layer):
    nd = len(shape)
    return pl.BlockSpec((None,) + tuple(shape), lambda i: (layer,) + (0,) * nd,
                        pipeline_mode=pl.Buffered(1))


def _ffn_body(x, g, mod_ref, base, wg_ref, wu_ref, wd_ref):
    h = _norm_mod(x, g, mod_ref[base:base + 1, :], mod_ref[base + 1:base + 2, :]).astype(BF16)
    d_ff = wg_ref.shape[-1]
    acc = jnp.zeros(x.shape, F32)
    for c0 in range(0, d_ff, FF_CHUNK):
        c1 = min(c0 + FF_CHUNK, d_ff)
        gate = _dot(h, wg_ref[:, c0:c1])
        up = _dot(h, wu_ref[:, c0:c1])
        act = (_silu(gate) * up).astype(BF16)
        acc = acc + _dot(act, wd_ref[c0:c1, :])
    return x + (0.5 * mod_ref[base + 2:base + 3, :]) * acc


def _rows_specs(tl, rows, width, first_tile):
    tm = tl.tm
    if not isinstance(rows, tuple):
        return [pl.BlockSpec((tm, width), lambda i: (i, 0))]
    assert first_tile == 0
    return [pl.BlockSpec((tm, width), lambda i: (jnp.minimum(i, tl.ctx_tiles - 1), 0)),
            pl.BlockSpec((tm, width), lambda i: (jnp.maximum(i - tl.ctx_tiles, 0), 0))]


def _rows_load(tl, refs):
    if len(refs) == 1:
        return refs[0][...]
    return jnp.where(pl.program_id(0) < tl.ctx_tiles, refs[0][...], refs[1][...])


def _rows_args(rows):
    return rows if isinstance(rows, tuple) else (rows,)


def _ffn_kernel(*refs, base, tl, n_x):
    g_ref, mod_ref, wg_ref, wu_ref, wd_ref, o_ref = refs[n_x:]
    o_ref[...] = _ffn_body(_rows_load(tl, refs[:n_x]), g_ref[...], mod_ref, base, wg_ref, wu_ref, wd_ref)


def _ffn(tl, xs, mods, g_norm, wg, wu, wd, *, layer, n_tiles):
    d, d_ff = wg.shape[-2:]
    base, gi, wi, first_tile = 0, layer * 3, 0, 0
    x_args = _rows_args(xs)
    return pl.pallas_call(
        functools.partial(_ffn_kernel, base=base, tl=tl, n_x=len(x_args)),
        out_shape=jax.ShapeDtypeStruct((n_tiles * tl.tm, d), F32),
        grid=(n_tiles,),
        in_specs=_rows_specs(tl, xs, d, first_tile) + [
            _layer_vec_spec(gi, d),
            _mod_spec(tl, layer, first_tile, d),
            _resident((d, d_ff), wi),
            _resident((d, d_ff), wi),
            _resident((d_ff, d), wi),
        ],
        out_specs=pl.BlockSpec((tl.tm, d), lambda i: (i, 0)),
        compiler_params=pltpu.CompilerParams(
            dimension_semantics=("arbitrary",), vmem_limit_bytes=VMEM_LIMIT),
        name="ffn",
    )(*x_args, g_norm, mods, wg, wu, wd)


def _mix_pre_kernel(x_ref, g_ref, mod_ref, w_in_ref, gqk_ref, gcq_ref, gckv_ref, wuq_ref, wukv_ref,
                    gqm_ref, gkm_ref, gkr_ref, bd_a_ref, bd_q_ref, bd_k_ref, tab_a_ref, tab_m_ref,
                    qa_ref, ka_ref, va_ref, u_ref, bg_ref, qm_ref, km_ref, vm_ref):
    x = x_ref[...]
    h = _norm_mod(x, g_ref[...], mod_ref[3:4, :], mod_ref[4:5, :]).astype(BF16)
    lane = lax.broadcasted_iota(jnp.int32, (x.shape[0], LANES), 1)
    low = lane < GQA_HEAD_DIM
    one_hot0 = (lane == 0).astype(F32)

    pa = _dot(h, w_in_ref[:, 0:GQA_IN])
    qk = pa[:, 0:GQA_Q + GQA_KV]
    qk = qk * lax.rsqrt(_group_mean_sq(qk, bd_a_ref) + EPS) * gqk_ref[...]
    c0, c1, c2, kcol = _rope_cols(qk, tab_a_ref, GQA_HEAD_DIM // 4)
    zero = jnp.zeros_like(c0)
    slots = [
        jnp.where(low, c0, zero),
        jnp.where(low, pltpu.roll(c0, GQA_HEAD_DIM, axis=1), zero),
        jnp.where(low, c1, zero),
        jnp.where(low, zero, c1),
        jnp.where(low, zero, pltpu.roll(c2, GQA_HEAD_DIM, axis=1)),
        jnp.where(low, zero, c2),
    ]
    qa_ref[...] = jnp.concatenate(slots, axis=-1).astype(BF16)
    ka_ref[...] = kcol.astype(BF16)
    vcol = pa[:, GQA_Q + GQA_KV:GQA_IN]
    va_ref[...] = jnp.concatenate(
        [jnp.where(low, one_hot0, pltpu.roll(vcol, GQA_HEAD_DIM, axis=1)),
         jnp.where(low, one_hot0, vcol)], axis=-1).T.astype(BF16)

    ps = _dot(h, w_in_ref[:, GQA_IN:GQA_IN + CONV_IN])
    u_ref[...] = ps[:, 2 * CONV_DIM:3 * CONV_DIM] * ps[:, 0:CONV_DIM]
    bg_ref[...] = ps[:, CONV_DIM:2 * CONV_DIM]

    pm = _dot(h, w_in_ref[:, GQA_IN + CONV_IN:GQA_IN + CONV_IN + MLA_IN_PAD])
    cq = pm[:, 0:MLA_Q_LORA]
    cq = (cq * _row_rms_scale(cq, MLA_Q_LORA) * gcq_ref[...]).astype(BF16)
    ckv = pm[:, MLA_Q_LORA:MLA_Q_LORA + MLA_KV_LORA]
    ckv = (ckv * _row_rms_scale(ckv, MLA_KV_LORA) * gckv_ref[...]).astype(BF16)
    q = _dot(cq, wuq_ref[...])
    q = q * lax.rsqrt(_group_mean_sq(q, bd_q_ref) + EPS) * gqm_ref[...]
    qm_ref[...] = jnp.concatenate(_rope_cols(q, tab_m_ref, MLA_ROPE // 4), axis=-1).astype(BF16)
    kr = pm[:, MLA_Q_LORA + MLA_KV_LORA:MLA_IN_PAD]
    kr = kr * _row_rms_scale(kr, MLA_ROPE) * gkr_ref[...]
    kr = _rope_cols(pltpu.roll(kr, MLA_NOPE, axis=1), tab_m_ref, MLA_ROPE // 4)[0]
    kv = _dot(ckv, wukv_ref[...])
    kn = kv * lax.rsqrt(_group_mean_sq(kv, bd_k_ref) + EPS) * gkm_ref[...]
    km_ref[...] = jnp.concatenate(
        [jnp.where(low, kn[:, c:c + LANES], kr) for c in range(0, kn.shape[-1], LANES)],
        axis=-1).astype(BF16)
    vm_ref[...] = jnp.concatenate(
        [jnp.where(low, one_hot0, kv[:, c:c + LANES]) for c in range(0, kv.shape[-1], LANES)],
        axis=-1).T.astype(BF16)


def _mix_pre(tl, xs, mods, g_norm, p, *, layer):
    d = xs.shape[-1]
    tm = tl.tm
    n_tiles = tl.ctx_tiles + tl.lat_tiles
    t = tl.n_rows
    hs = N_HEADS * HEAD_SLOT

    def const2(shape):
        return pl.BlockSpec(shape, lambda i: (0, 0))

    def rope_spec():
        return pl.BlockSpec((3, tm, LANES), lambda i: (0, tl.rope_block(i), 0))

    def out(width):
        return pl.BlockSpec((tm, width), lambda i: (i, 0))

    def out_t(width):
        return pl.BlockSpec((width, tm), lambda i: (0, i))

    w_in = p["w_in"]
    return pl.pallas_call(
        _mix_pre_kernel,
        out_shape=(
            jax.ShapeDtypeStruct((t, hs), BF16),
            jax.ShapeDtypeStruct((t, LANES), BF16),
            jax.ShapeDtypeStruct((2 * LANES, t), BF16),
            jax.ShapeDtypeStruct((t, CONV_DIM), F32),
            jax.ShapeDtypeStruct((t, CONV_DIM), F32),
            jax.ShapeDtypeStruct((t, hs), BF16),
            jax.ShapeDtypeStruct((t, hs), BF16),
            jax.ShapeDtypeStruct((hs, t), BF16),
        ),
        grid=(n_tiles,),
        in_specs=[
            pl.BlockSpec((tm, d), lambda i: (i, 0)),
            _layer_vec_spec(layer * 3 + 1, d),
            _mod_spec(tl, layer, 0, d),
            _resident(w_in.shape[1:], layer),
            _layer_vec_spec(layer, GQA_Q + GQA_KV),
            _layer_vec_spec(layer, MLA_Q_LORA),
            _layer_vec_spec(layer, MLA_KV_LORA),
            _resident(p["w_uq"].shape[1:], layer),
            _resident(p["w_ukv"].shape[1:], layer),
            _layer_vec_spec(layer, hs),
            _layer_vec_spec(layer, hs),
            _layer_vec_spec(layer, LANES),
            const2((MXU_DIM, MXU_DIM)),
            const2((MXU_DIM, MXU_DIM)),
            const2((MXU_DIM, MXU_DIM)),
            rope_spec(),
            rope_spec(),
        ],
        out_specs=(out(hs), out(LANES), out_t(2 * LANES), out(CONV_DIM), out(CONV_DIM),
                   out(hs), out(hs), out_t(hs)),
        compiler_params=pltpu.CompilerParams(
            dimension_semantics=("arbitrary",), vmem_limit_bytes=VMEM_LIMIT),
        name="mix_pre",
    )(xs, g_norm, mods, w_in, p["g_qk"], p["g_cq"], p["g_ckv"], p["w_uq"], p["w_ukv"],
      p["g_qm"], p["g_km"], p["g_kr"], p["bd_a"], p["bd_q"], p["bd_k"], p["tab_a"], p["tab_m"])


def _attend(q_ref, k_refs, vt_refs, o_ref, k_slot, v_slot):
    sub = min(Q_SUB, q_ref.shape[0])
    units = [(r, h) for r in range(0, q_ref.shape[0], sub) for h in range(N_HEADS)]

    def scores(unit):
        r, h = unit
        q = q_ref[r:r + sub, h * HEAD_SLOT:(h + 1) * HEAD_SLOT]
        ks = k_slot[h] * HEAD_SLOT
        return [_dot_nt(k_ref[:, ks:ks + HEAD_SLOT], q) for k_ref in k_refs]

    def weighted_values(unit, st):
        vs = v_slot[unit[1]] * HEAD_SLOT
        m = jnp.max(st[0], axis=0, keepdims=True)
        for sp in st[1:]:
            m = jnp.maximum(m, jnp.max(sp, axis=0, keepdims=True))
        acc = None
        for sp, vt_ref in zip(st, vt_refs):
            pv = _dot(vt_ref[vs:vs + HEAD_SLOT, :], jnp.exp2(sp - m).astype(BF16))
            acc = pv if acc is None else acc + pv
        return acc[MLA_V:, :] / acc[0:1, :]

    outs = []
    st = scores(units[0])
    for n, unit in enumerate(units):
        st_next = scores(units[n + 1]) if n + 1 < len(units) else None
        outs.append(weighted_values(unit, st))
        st = st_next
        if unit[1] == N_HEADS - 1:
            o_ref[unit[0]:unit[0] + sub, :] = jnp.concatenate(outs, axis=0).T.astype(BF16)
            outs = []


def _attn_kernel(*refs, nq, ctx_queries, n_cast, k_slot, v_slot):
    n_in = (6 if ctx_queries else 5) + n_cast
    cast_in, cast_out = refs[n_in - n_cast:n_in], refs[len(refs) - n_cast:]
    for w_ref, wb_ref in zip(cast_in, cast_out):
        wb_ref[...] = w_ref[...].astype(BF16)
    if ctx_queries:
        q_ref, qc_ref, kc_ref, vtc_ref, kl_ref, vtl_ref = refs[:6]
        o_ref, oc_ref = refs[n_in:n_in + 2]
    else:
        q_ref, kc_ref, vtc_ref, kl_ref, vtl_ref = refs[:5]
        o_ref = refs[n_in]
    step = pl.program_id(1)

    @pl.when(step >= (1 if ctx_queries else 0))
    def _():
        _attend(q_ref, (kc_ref, kl_ref), (vtc_ref, vtl_ref), o_ref, k_slot, v_slot)

    if ctx_queries:
        @pl.when(step == 0)
        def _():
            _attend(qc_ref, (kc_ref,), (vtc_ref,), oc_ref, k_slot, v_slot)


def _attention(tl, q, k, v, *, k_slot, v_slot, ctx_queries, cast=()):
    b, ctx_len, seq = tl.batch, tl.ctx_len, tl.seq
    kw, vw = k.shape[-1], v.shape[0]
    lat_block0 = tl.n_ctx_rows // seq
    tq = min(Q_TILE, seq)
    nq = seq // tq
    q_block0 = tl.n_ctx_rows // tq
    qw, ow = N_HEADS * HEAD_SLOT, N_HEADS * MLA_V

    first_lat = 1 if ctx_queries else 0

    def lat_tile(bi, qi):
        return bi * nq + jnp.maximum(qi - first_lat, 0)

    q_specs = [pl.BlockSpec((tq, qw), lambda bi, qi: (q_block0 + lat_tile(bi, qi), 0))]
    out_specs = [pl.BlockSpec((tq, ow), lambda bi, qi: (lat_tile(bi, qi), 0))]
    out_shape = [jax.ShapeDtypeStruct((b * seq, ow), BF16)]
    operands = [q]
    if ctx_queries:
        q_specs.append(pl.BlockSpec((ctx_len, qw), lambda bi, qi: (bi, 0)))
        out_specs.append(pl.BlockSpec((ctx_len, ow), lambda bi, qi: (bi, 0)))
        out_shape.append(jax.ShapeDtypeStruct((b * ctx_len, ow), BF16))
        operands.append(q)
    kv_specs = [
        pl.BlockSpec((ctx_len, kw), lambda bi, qi: (bi, 0)),
        pl.BlockSpec((vw, ctx_len), lambda bi, qi: (0, bi)),
        pl.BlockSpec((seq, kw), lambda bi, qi: (lat_block0 + bi, 0)),
        pl.BlockSpec((vw, seq), lambda bi, qi: (0, lat_block0 + bi)),
    ]
    steps = nq + first_lat
    cast_in, cast_out = [], []
    for w, layer in cast:
        rows, cols = w.shape[1:]
        chunk = rows // (b * steps)
        assert chunk * b * steps == rows and chunk % (2 * SUBLANES) == 0
        cast_in.append(pl.BlockSpec((None, chunk, cols), lambda bi, qi, layer=layer: (layer, bi * steps + qi, 0)))
        cast_out.append(pl.BlockSpec((None, chunk, cols), lambda bi, qi: (0, bi * steps + qi, 0)))
        out_shape.append(jax.ShapeDtypeStruct((1, rows, cols), BF16))
    outs = pl.pallas_call(
        functools.partial(_attn_kernel, nq=nq, ctx_queries=ctx_queries, n_cast=len(cast),
                          k_slot=k_slot, v_slot=v_slot),
        out_shape=out_shape,
        grid=(b, steps),
        in_specs=q_specs + kv_specs + cast_in,
        out_specs=out_specs + cast_out,
        compiler_params=pltpu.CompilerParams(
            dimension_semantics=("arbitrary", "arbitrary"), vmem_limit_bytes=VMEM_LIMIT),
        name="attention",
    )(*operands, k, v, k, v, *[w for w, _ in cast])
    n_o = 1 + first_lat
    return outs[0], (outs[1] if ctx_queries else None), tuple(outs[n_o:])


def _mod_const(v, n):
    return v & (n - 1) if n & (n - 1) == 0 else lax.rem(v, n)

def _mix_post_kernel(*refs, tl, first_tile, n_y):
    ya_refs, ym_refs = refs[:n_y], refs[n_y:2 * n_y]
    (x_ref, mod_ref, u_ref, up_ref, un_ref, bg_ref, cw_ref, cb_ref, w_out_ref,
     g_ref, wg_ref, wu_ref, wd_ref, o_ref) = refs[2 * n_y:]
    tm = x_ref.shape[0]
    i = pl.program_id(0) + first_tile
    is_ctx = i < tl.ctx_tiles
    seq_len = jnp.where(is_ctx, tl.ctx_len, tl.seq)
    local = lax.broadcasted_iota(jnp.int32, (tm, CONV_DIM), 0)
    row = local + i * tm
    pos = jnp.where(is_ctx, _mod_const(row, tl.ctx_len), _mod_const(row, tl.seq))
    u = u_ref[...]
    prev = jnp.where(local == 0, up_ref[SUBLANES - 1:SUBLANES, :], pltpu.roll(u, 1, axis=0))
    prev = jnp.where(pos == 0, 0.0, prev)
    nxt = jnp.where(local == tm - 1, un_ref[0:1, :], pltpu.roll(u, tm - 1, axis=0))
    nxt = jnp.where(pos == seq_len - 1, 0.0, nxt)
    y = prev * cw_ref[0:1, :] + u * cw_ref[1:2, :] + nxt * cw_ref[2:3, :] + cb_ref[...]
    yc = (bg_ref[...] * y).astype(BF16)
    mixed = jnp.concatenate([_rows_load(tl, ya_refs), yc, _rows_load(tl, ym_refs)], axis=-1)
    x = x_ref[...] + mod_ref[5:6, :] * _dot(mixed, w_out_ref[...])
    o_ref[...] = _ffn_body(x, g_ref[...], mod_ref, 6, wg_ref, wu_ref, wd_ref)


def _mix_post(tl, xs, mods, ya, ym, u, bg, p, g_norm, wg, wu, wd, *, layer, first_tile, n_tiles):
    d = xs.shape[-1]
    d_ff = wg.shape[-1]
    tm = tl.tm
    rows8 = tm // SUBLANES
    last8 = tl.n_rows // SUBLANES - 1

    def tile(width):
        return pl.BlockSpec((tm, width), lambda i: (i + first_tile, 0))

    ya_args, ym_args = _rows_args(ya), _rows_args(ym)
    return pl.pallas_call(
        functools.partial(_mix_post_kernel, tl=tl, first_tile=first_tile, n_y=len(ya_args)),
        out_shape=jax.ShapeDtypeStruct((n_tiles * tm, d), F32),
        grid=(n_tiles,),
        in_specs=_rows_specs(tl, ya, GQA_Q, first_tile) + _rows_specs(tl, ym, N_HEADS * MLA_V, first_tile) + [
            tile(d),
            _mod_spec(tl, layer, first_tile, d),
            tile(CONV_DIM),
            pl.BlockSpec((SUBLANES, CONV_DIM),
                         lambda i: (jnp.maximum((i + first_tile) * rows8 - 1, 0), 0)),
            pl.BlockSpec((SUBLANES, CONV_DIM),
                         lambda i: (jnp.minimum((i + first_tile + 1) * rows8, last8), 0)),
            tile(CONV_DIM),
            pl.BlockSpec((None, CONV_WIDTH, CONV_DIM), lambda i: (layer, 0, 0)),
            _layer_vec_spec(layer, CONV_DIM),
            _resident(p["w_out"].shape[1:], layer),
            _layer_vec_spec(layer * 3 + 2, d),
            _resident((d, d_ff), 0),
            _resident((d, d_ff), 0),
            _resident((d_ff, d), 0),
        ],
        out_specs=pl.BlockSpec((tm, d), lambda i: (i, 0)),
        compiler_params=pltpu.CompilerParams(
            dimension_semantics=("arbitrary",), vmem_limit_bytes=VMEM_LIMIT),
        name="mix_post",
    )(*ya_args, *ym_args, xs, mods, u, u, u, bg, p["conv_w"], p["conv_b"], p["w_out"],
      g_norm, wg, wu, wd)


def _block_diag_mean(pattern):
    m = jnp.zeros((MXU_DIM, MXU_DIM), F32)
    for base in range(0, MXU_DIM, LANES):
        o = base
        for width, is_group in pattern:
            if is_group:
                m = m.at[o:o + width, o:o + width].set(1.0 / width)
            o += width
    return m.astype(BF16)


def _rope_tables(seq, tm, head_dim, lane_offset):
    rows = seq // GRID_W
    row = jnp.repeat(jnp.arange(rows, dtype=F32), GRID_W)
    col = jnp.tile(jnp.arange(GRID_W, dtype=F32), rows)
    half = head_dim // 2
    inv = 1.0 / (ROPE_THETA ** (jnp.arange(0, half, 2, dtype=F32) / half))
    ar = row[:, None] * inv[None, :]
    ac = col[:, None] * inv[None, :]
    ang = jnp.concatenate([ar, ar, ac, ac], axis=-1)
    cos, sin = jnp.cos(ang), jnp.sin(ang)
    quarter = head_dim // 4
    up_mask = (jnp.arange(head_dim) // quarter) % 2 == 0
    sin_up = jnp.where(up_mask, -sin, 0.0)
    sin_dn = jnp.where(up_mask, 0.0, sin)
    reps = (LANES - lane_offset) // head_dim if lane_offset == 0 else 1

    def place(t, fill):
        t = jnp.tile(t, (1, reps))
        left = jnp.full((seq, lane_offset), fill, F32)
        right = jnp.zeros((seq, LANES - lane_offset - t.shape[1]), F32)
        return jnp.concatenate([left, t, right], axis=-1)

    lat = jnp.stack([place(cos, 1.0), place(sin_up, 0.0), place(sin_dn, 0.0)])
    ident = jnp.stack([jnp.ones((tm, LANES), F32), jnp.zeros((tm, LANES), F32), jnp.zeros((tm, LANES), F32)])
    if lane_offset:
        keep = (jnp.arange(LANES) < lane_offset + head_dim).astype(F32)
        ident = ident * keep
    return jnp.concatenate([ident, lat], axis=1)


def _prepare(tl, w_in, w_out, gqa_g_q, gqa_g_k, conv_w, conv_b, mla_g_cq, mla_g_ckv, mla_w_uq,
             mla_w_ukv, mla_g_qn, mla_g_kn, mla_g_qr, mla_g_kr):
    depth, d, _ = w_in.shape
    pad = MLA_IN_PAD - MLA_IN
    w_in_p = jnp.concatenate([w_in, jnp.zeros((depth, d, pad), w_in.dtype)], axis=-1).astype(BF16)
    dq = MLA_NOPE + MLA_ROPE
    wq = mla_w_uq.reshape(depth, MLA_Q_LORA, MLA_HEADS, dq)
    wq = jnp.concatenate([wq, jnp.zeros((depth, MLA_Q_LORA, MLA_HEADS, HEAD_SLOT - dq), wq.dtype)], axis=-1)
    w_uq = wq.reshape(depth, MLA_Q_LORA, MLA_HEADS * HEAD_SLOT).astype(BF16)

    def vec(a):
        return a.reshape(depth, 1, a.shape[-1]).astype(F32)

    zeros32 = jnp.zeros((depth, HEAD_SLOT - dq), F32)
    mla_scale = dq ** -0.5 * LOG2_E
    g_qm = jnp.tile(jnp.concatenate([mla_g_qn * mla_scale, mla_g_qr * mla_scale, zeros32], axis=-1),
                    (1, MLA_HEADS))
    g_km = jnp.tile(jnp.concatenate([mla_g_kn, jnp.zeros((depth, MLA_V), F32)], axis=-1), (1, MLA_HEADS))
    g_kr = jnp.concatenate([mla_g_kr, jnp.zeros((depth, LANES - MLA_ROPE), F32)], axis=-1)
    gqa_scale = GQA_HEAD_DIM ** -0.5 * LOG2_E
    g_qk = jnp.concatenate([jnp.tile(gqa_g_q * gqa_scale, (1, GQA_HEADS)),
                            jnp.tile(gqa_g_k, (1, GQA_KV_HEADS))], axis=-1)
    return {
        "w_in": w_in_p,
        "w_out": w_out.astype(BF16),
        "w_uq": w_uq,
        "w_ukv": mla_w_ukv.astype(BF16),
        "g_qk": vec(g_qk), "g_cq": vec(mla_g_cq), "g_ckv": vec(mla_g_ckv),
        "g_qm": vec(g_qm), "g_km": vec(g_km), "g_kr": vec(g_kr),
        "conv_w": conv_w.astype(F32), "conv_b": vec(conv_b),
        "bd_a": _block_diag_mean([(64, True), (64, True)]),
        "bd_q": _block_diag_mean([(MLA_NOPE, True), (MLA_ROPE, True)]),
        "bd_k": _block_diag_mean([(MLA_NOPE, True)]),
        "tab_a": _rope_tables(tl.seq, tl.tm, GQA_HEAD_DIM, 0),
        "tab_m": _rope_tables(tl.seq, tl.tm, MLA_ROPE, MLA_NOPE),
    }


GQA_K_SLOT = (0,) * N_HEADS
GQA_V_SLOT = tuple(h // (GQA_HEADS // GQA_KV_HEADS) for h in range(N_HEADS))
MLA_SLOT = tuple(range(N_HEADS))


def kernel(x, c, ctx, c_ctx, w_mod, b_mod, g_norm, ffn_w_gate, ffn_w_up, ffn_w_down, w_in, w_out,
           gqa_g_q, gqa_g_k, conv_w, conv_b, mla_g_cq, mla_g_ckv, mla_w_uq, mla_w_ukv, mla_g_qn,
           mla_g_kn, mla_g_qr, mla_g_kr):
    batch, seq, d = x.shape
    ctx_len = ctx.shape[1]
    depth = w_mod.shape[0]
    d_ff = ffn_w_gate.shape[-1]
    assert d == 1024 and batch + 1 <= MOD_ROWS and seq % GRID_W == 0
    tl = _Tiling(batch, ctx_len, seq, TOKEN_TILE)
    tl_ffn = _Tiling(batch, ctx_len, seq, FFN_TILE)
    n_all = tl.ctx_tiles + tl.lat_tiles

    cvec = jnp.concatenate([c_ctx[None, :], c, jnp.zeros((MOD_ROWS - 1 - batch, d), F32)], axis=0)
    mods = _modulation(cvec, w_mod, b_mod).reshape(depth, MOD_ROWS, N_MOD, d)
    p = _prepare(tl, w_in, w_out, gqa_g_q, gqa_g_k, conv_w, conv_b, mla_g_cq, mla_g_ckv, mla_w_uq,
                 mla_w_ukv, mla_g_qn, mla_g_kn, mla_g_qr, mla_g_kr)
    g3 = g_norm.reshape(depth * 3, 1, d)
    ffn_f32 = (ffn_w_gate.reshape(depth * 2, d, d_ff), ffn_w_up.reshape(depth * 2, d, d_ff),
               ffn_w_down.reshape(depth * 2, d_ff, d))
    ffn_w = tuple(w[0:1].astype(BF16) for w in ffn_f32)

    xs = (ctx.reshape(batch * ctx_len, d), x.reshape(batch * seq, d))
    for layer in range(depth):
        need_ctx = layer < depth - 1
        xs = _ffn(tl_ffn, xs, mods, g3, *ffn_w, layer=layer, n_tiles=tl_ffn.ctx_tiles + tl_ffn.lat_tiles)
        qa, ka, va, u, bg, qm, km, vm = _mix_pre(tl, xs, mods, g3, p, layer=layer)
        ya, ya_c, ffn_w = _attention(tl, qa, ka, va, k_slot=GQA_K_SLOT, v_slot=GQA_V_SLOT, ctx_queries=need_ctx,
                                     cast=[(w, 2 * layer + 1) for w in ffn_f32])
        ym, ym_c, ffn_w_next = _attention(tl, qm, km, vm, k_slot=MLA_SLOT, v_slot=MLA_SLOT, ctx_queries=need_ctx,
                                          cast=[(w, 2 * layer + 2) for w in ffn_f32] if need_ctx else ())
        if need_ctx:
            ya, ym = (ya_c, ya), (ym_c, ym)
            first, n_tiles = 0, n_all
        else:
            first, n_tiles = tl.ctx_tiles, tl.lat_tiles
        xs = _mix_post(tl, xs, mods, ya, ym, u, bg, p, g3, *ffn_w,
                       layer=layer, first_tile=first, n_tiles=n_tiles)
        ffn_w = ffn_w_next
    return xs.reshape(batch, seq, d)
```

```python
import functools

import jax
import jax.numpy as jnp
from jax import lax
from jax.experimental import pallas as pl
from jax.experimental.pallas import tpu as pltpu

F32 = jnp.float32
BF16 = jnp.bfloat16

GRID_W = 64
N_MOD = 9
EPS = 1e-6
ROPE_THETA = 10000.0
LOG2_E = 1.4426950408889634
GQA_HEADS = 6
GQA_KV_HEADS = 2
GQA_HEAD_DIM = 64
GQA_Q = GQA_HEADS * GQA_HEAD_DIM
GQA_KV = GQA_KV_HEADS * GQA_HEAD_DIM
GQA_IN = GQA_Q + 2 * GQA_KV
CONV_DIM = 256
CONV_WIDTH = 3
CONV_IN = 3 * CONV_DIM
MLA_HEADS = 6
MLA_Q_LORA = 384
MLA_KV_LORA = 256
MLA_NOPE = 64
MLA_ROPE = 32
MLA_V = 64
MLA_IN = MLA_Q_LORA + MLA_KV_LORA + MLA_ROPE

LANES = 128
SUBLANES = 8
MXU_DIM = 256
HEAD_SLOT = LANES
N_HEADS = 6
MLA_IN_PAD = 768
MOD_ROWS = 16
VMEM_LIMIT = 56 * 1024 * 1024
TOKEN_TILE = 512
FFN_TILE = 1024
Q_TILE = 2048
Q_SUB = 1024
FF_CHUNK = 512


def _dot(a, b):
    return jnp.dot(a, b, preferred_element_type=F32)


def _dot_nt(a, b):
    return lax.dot_general(a, b, (((1,), (1,)), ((), ())), preferred_element_type=F32)


def _silu(x):
    return x * jax.nn.sigmoid(x)


def _row_rms_scale(x, width):
    return lax.rsqrt(jnp.sum(x * x, axis=-1, keepdims=True) * (1.0 / width) + EPS)


def _norm_mod(x, g, shift, scale):
    y = x * _row_rms_scale(x, x.shape[-1])
    return (y * g) * (1.0 + scale) + shift


def _group_mean_sq(x, bd_ref):
    cols = []
    bd = bd_ref[...]
    for c in range(0, x.shape[-1], MXU_DIM):
        sq = x[:, c:c + MXU_DIM]
        cols.append(_dot((sq * sq).astype(BF16), bd))
    return jnp.concatenate(cols, axis=-1)


def _rope_cols(x, tab_ref, quarter):
    cos, sin_up, sin_dn = tab_ref[0], tab_ref[1], tab_ref[2]
    cols = []
    for c in range(0, x.shape[-1], LANES):
        v = x[:, c:c + LANES]
        up = pltpu.roll(v, LANES - quarter, axis=1)
        dn = pltpu.roll(v, quarter, axis=1)
        cols.append(v * cos + up * sin_up + dn * sin_dn)
    return cols


def _mod_kernel(c_ref, w_ref, b_ref, o_ref):
    s = _silu(c_ref[...]).astype(BF16)
    o_ref[...] = _dot(s, w_ref[...].astype(BF16)) + b_ref[...]


def _modulation(cvec, w_mod, b_mod):
    depth, d, n = w_mod.shape
    tn = 1024
    return pl.pallas_call(
        _mod_kernel,
        out_shape=jax.ShapeDtypeStruct((depth, MOD_ROWS, n), F32),
        grid=(depth, n // tn),
        in_specs=[
            pl.BlockSpec((MOD_ROWS, d), lambda l, j: (0, 0)),
            pl.BlockSpec((None, d, tn), lambda l, j: (l, 0, j)),
            pl.BlockSpec((None, 1, tn), lambda l, j: (l, 0, j)),
        ],
        out_specs=pl.BlockSpec((None, MOD_ROWS, tn), lambda l, j: (l, 0, j)),
        compiler_params=pltpu.CompilerParams(
            dimension_semantics=("arbitrary", "arbitrary"), vmem_limit_bytes=VMEM_LIMIT),
        name="modulation",
    )(cvec, w_mod, b_mod.reshape(depth, 1, n))


class _Tiling:
    def __init__(self, batch, ctx_len, seq, tm):
        self.batch, self.ctx_len, self.seq = batch, ctx_len, seq
        self.n_ctx_rows = batch * ctx_len
        self.n_rows = self.n_ctx_rows + batch * seq
        while self.n_ctx_rows % tm or seq % tm:
            tm //= 2
        assert tm >= SUBLANES
        self.tm = tm
        self.ctx_tiles = self.n_ctx_rows // tm
        self.lat_tiles = batch * seq // tm
        self.tiles_per_seq = seq // tm
        assert self.n_ctx_rows % seq == 0

    def mod_row(self, i):
        return jnp.where(i < self.ctx_tiles, 0, 1 + (i - self.ctx_tiles) // self.tiles_per_seq)

    def rope_block(self, i):
        return jnp.where(i < self.ctx_tiles, 0, 1 + (i - self.ctx_tiles) % self.tiles_per_seq)


def _mod_spec(tl, layer, first_tile, d):
    return pl.BlockSpec((None, None, N_MOD, d), lambda i: (layer, tl.mod_row(i + first_tile), 0, 0))


def _layer_vec_spec(layer, width):
    return pl.BlockSpec((None, 1, width), lambda i: (layer, 0, 0))


def _resident(shape, layer):
    nd = len(shape)
    return pl.BlockSpec((None,) + tuple(shape), lambda i: (layer,) + (0,) * nd,
                        pipeline_mode=pl.Buffered(1))


def _ffn_body(x, g, mod_ref, base, wg_ref, wu_ref, wd_ref):
    h = _norm_mod(x, g, mod_ref[base:base + 1, :], mod_ref[base + 1:base + 2, :]).astype(BF16)
    d_ff = wg_ref.shape[-1]
    acc = jnp.zeros(x.shape, F32)
    for c0 in range(0, d_ff, FF_CHUNK):
        c1 = min(c0 + FF_CHUNK, d_ff)
        gate = _dot(h, wg_ref[:, c0:c1])
        up = _dot(h, wu_ref[:, c0:c1])
        act = (_silu(gate) * up).astype(BF16)
        acc = acc + _dot(act, wd_ref[c0:c1, :])
    return x + (0.5 * mod_ref[base + 2:base + 3, :]) * acc


def _rows_specs(tl, rows, width, first_tile):
    tm = tl.tm
    if not isinstance(rows, tuple):
        return [pl.BlockSpec((tm, width), lambda i: (i, 0))]
    assert first_tile == 0
    return [pl.BlockSpec((tm, width), lambda i: (jnp.minimum(i, tl.ctx_tiles - 1), 0)),
            pl.BlockSpec((tm, width), lambda i: (jnp.maximum(i - tl.ctx_tiles, 0), 0))]


def _rows_load(tl, refs):
    if len(refs) == 1:
        return refs[0][...]
    return jnp.where(pl.program_id(0) < tl.ctx_tiles, refs[0][...], refs[1][...])


def _rows_args(rows):
    return rows if isinstance(rows, tuple) else (rows,)


def _ffn_kernel(*refs, base, tl, n_x):
    g_ref, mod_ref, wg_ref, wu_ref, wd_ref, o_ref = refs[n_x:]
    o_ref[...] = _ffn_body(_rows_load(tl, refs[:n_x]), g_ref[...], mod_ref, base, wg_ref, wu_ref, wd_ref)


def _ffn(tl, xs, mods, g_norm, wg, wu, wd, *, layer, n_tiles):
    d, d_ff = wg.shape[-2:]
    base, gi, wi, first_tile = 0, layer * 3, 0, 0
    x_args = _rows_args(xs)
    return pl.pallas_call(
        functools.partial(_ffn_kernel, base=base, tl=tl, n_x=len(x_args)),
        out_shape=jax.ShapeDtypeStruct((n_tiles * tl.tm, d), F32),
        grid=(n_tiles,),
        in_specs=_rows_specs(tl, xs, d, first_tile) + [
            _layer_vec_spec(gi, d),
            _mod_spec(tl, layer, first_tile, d),
            _resident((d, d_ff), wi),
            _resident((d, d_ff), wi),
            _resident((d_ff, d), wi),
        ],
        out_specs=pl.BlockSpec((tl.tm, d), lambda i: (i, 0)),
        compiler_params=pltpu.CompilerParams(
            dimension_semantics=("arbitrary",), vmem_limit_bytes=VMEM_LIMIT),
        name="ffn",
    )(*x_args, g_norm, mods, wg, wu, wd)


def _mix_pre_kernel(x_ref, g_ref, mod_ref, w_in_ref, gqk_ref, gcq_ref, gckv_ref, wuq_ref, wukv_ref,
                    gqm_ref, gkm_ref, gkr_ref, bd_a_ref, bd_q_ref, bd_k_ref, tab_a_ref, tab_m_ref,
                    qa_ref, ka_ref, va_ref, u_ref, bg_ref, qm_ref, km_ref, vm_ref):
    x = x_ref[...]
    h = _norm_mod(x, g_ref[...], mod_ref[3:4, :], mod_ref[4:5, :]).astype(BF16)
    lane = lax.broadcasted_iota(jnp.int32, (x.shape[0], LANES), 1)
    low = lane < GQA_HEAD_DIM
    one_hot0 = (lane == 0).astype(F32)

    pa = _dot(h, w_in_ref[:, 0:GQA_IN])
    qk = pa[:, 0:GQA_Q + GQA_KV]
    qk = qk * lax.rsqrt(_group_mean_sq(qk, bd_a_ref) + EPS) * gqk_ref[...]
    c0, c1, c2, kcol = _rope_cols(qk, tab_a_ref, GQA_HEAD_DIM // 4)
    zero = jnp.zeros_like(c0)
    slots = [
        jnp.where(low, c0, zero),
        jnp.where(low, pltpu.roll(c0, GQA_HEAD_DIM, axis=1), zero),
        jnp.where(low, c1, zero),
        jnp.where(low, zero, c1),
        jnp.where(low, zero, pltpu.roll(c2, GQA_HEAD_DIM, axis=1)),
        jnp.where(low, zero, c2),
    ]
    qa_ref[...] = jnp.concatenate(slots, axis=-1).astype(BF16)
    ka_ref[...] = kcol.astype(BF16)
    vcol = pa[:, GQA_Q + GQA_KV:GQA_IN]
    va_ref[...] = jnp.concatenate(
        [jnp.where(low, one_hot0, pltpu.roll(vcol, GQA_HEAD_DIM, axis=1)),
         jnp.where(low, one_hot0, vcol)], axis=-1).T.astype(BF16)

    ps = _dot(h, w_in_ref[:, GQA_IN:GQA_IN + CONV_IN])
    u_ref[...] = ps[:, 2 * CONV_DIM:3 * CONV_DIM] * ps[:, 0:CONV_DIM]
    bg_ref[...] = ps[:, CONV_DIM:2 * CONV_DIM]

    pm = _dot(h, w_in_ref[:, GQA_IN + CONV_IN:GQA_IN + CONV_IN + MLA_IN_PAD])
    cq = pm[:, 0:MLA_Q_LORA]
    cq = (cq * _row_rms_scale(cq, MLA_Q_LORA) * gcq_ref[...]).astype(BF16)
    ckv = pm[:, MLA_Q_LORA:MLA_Q_LORA + MLA_KV_LORA]
    ckv = (ckv * _row_rms_scale(ckv, MLA_KV_LORA) * gckv_ref[...]).astype(BF16)
    q = _dot(cq, wuq_ref[...])
    q = q * lax.rsqrt(_group_mean_sq(q, bd_q_ref) + EPS) * gqm_ref[...]
    qm_ref[...] = jnp.concatenate(_rope_cols(q, tab_m_ref, MLA_ROPE // 4), axis=-1).astype(BF16)
    kr = pm[:, MLA_Q_LORA + MLA_KV_LORA:MLA_IN_PAD]
    kr = kr * _row_rms_scale(kr, MLA_ROPE) * gkr_ref[...]
    kr = _rope_cols(pltpu.roll(kr, MLA_NOPE, axis=1), tab_m_ref, MLA_ROPE // 4)[0]
    kv = _dot(ckv, wukv_ref[...])
    kn = kv * lax.rsqrt(_group_mean_sq(kv, bd_k_ref) + EPS) * gkm_ref[...]
    km_ref[...] = jnp.concatenate(
        [jnp.where(low, kn[:, c:c + LANES], kr) for c in range(0, kn.shape[-1], LANES)],
        axis=-1).astype(BF16)
    vm_ref[...] = jnp.concatenate(
        [jnp.where(low, one_hot0, kv[:, c:c + LANES]) for c in range(0, kv.shape[-1], LANES)],
        axis=-1).T.astype(BF16)


def _mix_pre(tl, xs, mods, g_norm, p, *, layer):
    d = xs.shape[-1]
    tm = tl.tm
    n_tiles = tl.ctx_tiles + tl.lat_tiles
    t = tl.n_rows
    hs = N_HEADS * HEAD_SLOT

    def const2(shape):
        return pl.BlockSpec(shape, lambda i: (0, 0))

    def rope_spec():
        return pl.BlockSpec((3, tm, LANES), lambda i: (0, tl.rope_block(i), 0))

    def out(width):
        return pl.BlockSpec((tm, width), lambda i: (i, 0))

    def out_t(width):
        return pl.BlockSpec((width, tm), lambda i: (0, i))

    w_in = p["w_in"]
    return pl.pallas_call(
        _mix_pre_kernel,
        out_shape=(
            jax.ShapeDtypeStruct((t, hs), BF16),
            jax.ShapeDtypeStruct((t, LANES), BF16),
            jax.ShapeDtypeStruct((2 * LANES, t), BF16),
            jax.ShapeDtypeStruct((t, CONV_DIM), F32),
            jax.ShapeDtypeStruct((t, CONV_DIM), F32),
            jax.ShapeDtypeStruct((t, hs), BF16),
            jax.ShapeDtypeStruct((t, hs), BF16),
            jax.ShapeDtypeStruct((hs, t), BF16),
        ),
        grid=(n_tiles,),
        in_specs=[
            pl.BlockSpec((tm, d), lambda i: (i, 0)),
            _layer_vec_spec(layer * 3 + 1, d),
            _mod_spec(tl, layer, 0, d),
            _resident(w_in.shape[1:], layer),
            _layer_vec_spec(layer, GQA_Q + GQA_KV),
            _layer_vec_spec(layer, MLA_Q_LORA),
            _layer_vec_spec(layer, MLA_KV_LORA),
            _resident(p["w_uq"].shape[1:], layer),
            _resident(p["w_ukv"].shape[1:], layer),
            _layer_vec_spec(layer, hs),
            _layer_vec_spec(layer, hs),
            _layer_vec_spec(layer, LANES),
            const2((MXU_DIM, MXU_DIM)),
            const2((MXU_DIM, MXU_DIM)),
            const2((MXU_DIM, MXU_DIM)),
            rope_spec(),
            rope_spec(),
        ],
        out_specs=(out(hs), out(LANES), out_t(2 * LANES), out(CONV_DIM), out(CONV_DIM),
                   out(hs), out(hs), out_t(hs)),
        compiler_params=pltpu.CompilerParams(
            dimension_semantics=("arbitrary",), vmem_limit_bytes=VMEM_LIMIT),
        name="mix_pre",
    )(xs, g_norm, mods, w_in, p["g_qk"], p["g_cq"], p["g_ckv"], p["w_uq"], p["w_ukv"],
      p["g_qm"], p["g_km"], p["g_kr"], p["bd_a"], p["bd_q"], p["bd_k"], p["tab_a"], p["tab_m"])


def _attend(jobs, k_slot, v_slot):
    units = []
    for job in jobs:
        sub = min(Q_SUB, job[0].shape[0])
        units += [(job, r, sub, h) for r in range(0, job[0].shape[0], sub) for h in range(N_HEADS)]

    def scores(unit):
        (q_ref, k_refs, _, _), r, sub, h = unit
        q = q_ref[r:r + sub, h * HEAD_SLOT:(h + 1) * HEAD_SLOT]
        ks = k_slot[h] * HEAD_SLOT
        return [_dot_nt(k_ref[:, ks:ks + HEAD_SLOT], q) for k_ref in k_refs]

    def weighted_values(unit, st):
        vt_refs = unit[0][2]
        vs = v_slot[unit[3]] * HEAD_SLOT
        m = jnp.max(st[0], axis=0, keepdims=True)
        for sp in st[1:]:
            m = jnp.maximum(m, jnp.max(sp, axis=0, keepdims=True))
        acc = None
        for sp, vt_ref in zip(st, vt_refs):
            pv = _dot(vt_ref[vs:vs + HEAD_SLOT, :], jnp.exp2(sp - m).astype(BF16))
            acc = pv if acc is None else acc + pv
        return acc[MLA_V:, :] / acc[0:1, :]

    outs = []
    st = scores(units[0])
    for n, unit in enumerate(units):
        st_next = scores(units[n + 1]) if n + 1 < len(units) else None
        outs.append(weighted_values(unit, st))
        st = st_next
        (_, _, _, o_ref), r, sub, h = unit
        if h == N_HEADS - 1:
            o_ref[r:r + sub, :] = jnp.concatenate(outs, axis=0).T.astype(BF16)
            outs = []


def _attn_kernel(*refs, nq, ctx_queries, n_cast, k_slot, v_slot):
    n_in = (6 if ctx_queries else 5) + n_cast
    cast_in, cast_out = refs[n_in - n_cast:n_in], refs[len(refs) - n_cast:]
    for w_ref, wb_ref in zip(cast_in, cast_out):
        wb_ref[...] = w_ref[...].astype(BF16)
    if ctx_queries:
        q_ref, qc_ref, kc_ref, vtc_ref, kl_ref, vtl_ref = refs[:6]
        o_ref, oc_ref = refs[n_in:n_in + 2]
    else:
        q_ref, kc_ref, vtc_ref, kl_ref, vtl_ref = refs[:5]
        o_ref = refs[n_in]
    lat_job = (q_ref, (kc_ref, kl_ref), (vtc_ref, vtl_ref), o_ref)
    if not ctx_queries:
        _attend([lat_job], k_slot, v_slot)
        return
    ctx_job = (qc_ref, (kc_ref,), (vtc_ref,), oc_ref)
    if nq == 1:
        _attend([ctx_job, lat_job], k_slot, v_slot)
        return
    step = pl.program_id(1)

    @pl.when(step == 0)
    def _():
        _attend([ctx_job, lat_job], k_slot, v_slot)

    @pl.when(step > 0)
    def _():
        _attend([lat_job], k_slot, v_slot)


def _attention(tl, q, k, v, *, k_slot, v_slot, ctx_queries, cast=()):
    b, ctx_len, seq = tl.batch, tl.ctx_len, tl.seq
    kw, vw = k.shape[-1], v.shape[0]
    lat_block0 = tl.n_ctx_rows // seq
    tq = min(Q_TILE, seq)
    nq = seq // tq
    q_block0 = tl.n_ctx_rows // tq
    qw, ow = N_HEADS * HEAD_SLOT, N_HEADS * MLA_V

    def lat_tile(bi, qi):
        return bi * nq + qi

    q_specs = [pl.BlockSpec((tq, qw), lambda bi, qi: (q_block0 + lat_tile(bi, qi), 0))]
    out_specs = [pl.BlockSpec((tq, ow), lambda bi, qi: (lat_tile(bi, qi), 0))]
    out_shape = [jax.ShapeDtypeStruct((b * seq, ow), BF16)]
    operands = [q]
    if ctx_queries:
        q_specs.append(pl.BlockSpec((ctx_len, qw), lambda bi, qi: (bi, 0)))
        out_specs.append(pl.BlockSpec((ctx_len, ow), lambda bi, qi: (bi, 0)))
        out_shape.append(jax.ShapeDtypeStruct((b * ctx_len, ow), BF16))
        operands.append(q)
    kv_specs = [
        pl.BlockSpec((ctx_len, kw), lambda bi, qi: (bi, 0)),
        pl.BlockSpec((vw, ctx_len), lambda bi, qi: (0, bi)),
        pl.BlockSpec((seq, kw), lambda bi, qi: (lat_block0 + bi, 0)),
        pl.BlockSpec((vw, seq), lambda bi, qi: (0, lat_block0 + bi)),
    ]
    steps = nq
    cast_in, cast_out = [], []
    for w, layer in cast:
        rows, cols = w.shape[1:]
        chunk = rows // (b * steps)
        assert chunk * b * steps == rows and chunk % (2 * SUBLANES) == 0
        cast_in.append(pl.BlockSpec((None, chunk, cols), lambda bi, qi, layer=layer: (layer, bi * steps + qi, 0)))
        cast_out.append(pl.BlockSpec((None, chunk, cols), lambda bi, qi: (0, bi * steps + qi, 0)))
        out_shape.append(jax.ShapeDtypeStruct((1, rows, cols), BF16))
    outs = pl.pallas_call(
        functools.partial(_attn_kernel, nq=nq, ctx_queries=ctx_queries, n_cast=len(cast),
                          k_slot=k_slot, v_slot=v_slot),
        out_shape=out_shape,
        grid=(b, steps),
        in_specs=q_specs + kv_specs + cast_in,
        out_specs=out_specs + cast_out,
        compiler_params=pltpu.CompilerParams(
            dimension_semantics=("arbitrary", "arbitrary"), vmem_limit_bytes=VMEM_LIMIT),
        name="attention",
    )(*operands, k, v, k, v, *[w for w, _ in cast])
    n_o = 2 if ctx_queries else 1
    return outs[0], (outs[1] if ctx_queries else None), tuple(outs[n_o:])


def _mod_const(v, n):
    return v & (n - 1) if n & (n - 1) == 0 else lax.rem(v, n)

def _mix_post_kernel(*refs, tl, first_tile, n_y):
    ya_refs, ym_refs = refs[:n_y], refs[n_y:2 * n_y]
    (x_ref, mod_ref, u_ref, up_ref, un_ref, bg_ref, cw_ref, cb_ref, w_out_ref,
     g_ref, wg_ref, wu_ref, wd_ref, o_ref) = refs[2 * n_y:]
    tm = x_ref.shape[0]
    i = pl.program_id(0) + first_tile
    is_ctx = i < tl.ctx_tiles
    seq_len = jnp.where(is_ctx, tl.ctx_len, tl.seq)
    local = lax.broadcasted_iota(jnp.int32, (tm, CONV_DIM), 0)
    row = local + i * tm
    pos = jnp.where(is_ctx, _mod_const(row, tl.ctx_len), _mod_const(row, tl.seq))
    u = u_ref[...]
    prev = jnp.where(local == 0, up_ref[SUBLANES - 1:SUBLANES, :], pltpu.roll(u, 1, axis=0))
    prev = jnp.where(pos == 0, 0.0, prev)
    nxt = jnp.where(local == tm - 1, un_ref[0:1, :], pltpu.roll(u, tm - 1, axis=0))
    nxt = jnp.where(pos == seq_len - 1, 0.0, nxt)
    y = prev * cw_ref[0:1, :] + u * cw_ref[1:2, :] + nxt * cw_ref[2:3, :] + cb_ref[...]
    yc = (bg_ref[...] * y).astype(BF16)
    mixed = jnp.concatenate([_rows_load(tl, ya_refs), yc, _rows_load(tl, ym_refs)], axis=-1)
    x = x_ref[...] + mod_ref[5:6, :] * _dot(mixed, w_out_ref[...])
    o_ref[...] = _ffn_body(x, g_ref[...], mod_ref, 6, wg_ref, wu_ref, wd_ref)


def _mix_post(tl, xs, mods, ya, ym, u, bg, p, g_norm, wg, wu, wd, *, layer, first_tile, n_tiles):
    d = xs.shape[-1]
    d_ff = wg.shape[-1]
    tm = tl.tm
    rows8 = tm // SUBLANES
    last8 = tl.n_rows // SUBLANES - 1

    def tile(width):
        return pl.BlockSpec((tm, width), lambda i: (i + first_tile, 0))

    ya_args, ym_args = _rows_args(ya), _rows_args(ym)
    return pl.pallas_call(
        functools.partial(_mix_post_kernel, tl=tl, first_tile=first_tile, n_y=len(ya_args)),
        out_shape=jax.ShapeDtypeStruct((n_tiles * tm, d), F32),
        grid=(n_tiles,),
        in_specs=_rows_specs(tl, ya, GQA_Q, first_tile) + _rows_specs(tl, ym, N_HEADS * MLA_V, first_tile) + [
            tile(d),
            _mod_spec(tl, layer, first_tile, d),
            tile(CONV_DIM),
            pl.BlockSpec((SUBLANES, CONV_DIM),
                         lambda i: (jnp.maximum((i + first_tile) * rows8 - 1, 0), 0)),
            pl.BlockSpec((SUBLANES, CONV_DIM),
                         lambda i: (jnp.minimum((i + first_tile + 1) * rows8, last8), 0)),
            tile(CONV_DIM),
            pl.BlockSpec((None, CONV_WIDTH, CONV_DIM), lambda i: (layer, 0, 0)),
            _layer_vec_spec(layer, CONV_DIM),
            _resident(p["w_out"].shape[1:], layer),
            _layer_vec_spec(layer * 3 + 2, d),
            _resident((d, d_ff), 0),
            _resident((d, d_ff), 0),
            _resident((d_ff, d), 0),
        ],
        out_specs=pl.BlockSpec((tm, d), lambda i: (i, 0)),
        compiler_params=pltpu.CompilerParams(
            dimension_semantics=("arbitrary",), vmem_limit_bytes=VMEM_LIMIT),
        name="mix_post",
    )(*ya_args, *ym_args, xs, mods, u, u, u, bg, p["conv_w"], p["conv_b"], p["w_out"],
      g_norm, wg, wu, wd)


def _block_diag_mean(pattern):
    m = jnp.zeros((MXU_DIM, MXU_DIM), F32)
    for base in range(0, MXU_DIM, LANES):
        o = base
        for width, is_group in pattern:
            if is_group:
                m = m.at[o:o + width, o:o + width].set(1.0 / width)
            o += width
    return m.astype(BF16)


def _rope_tables(seq, tm, head_dim, lane_offset):
    rows = seq // GRID_W
    row = jnp.repeat(jnp.arange(rows, dtype=F32), GRID_W)
    col = jnp.tile(jnp.arange(GRID_W, dtype=F32), rows)
    half = head_dim // 2
    inv = 1.0 / (ROPE_THETA ** (jnp.arange(0, half, 2, dtype=F32) / half))
    ar = row[:, None] * inv[None, :]
    ac = col[:, None] * inv[None, :]
    ang = jnp.concatenate([ar, ar, ac, ac], axis=-1)
    cos, sin = jnp.cos(ang), jnp.sin(ang)
    quarter = head_dim // 4
    up_mask = (jnp.arange(head_dim) // quarter) % 2 == 0
    sin_up = jnp.where(up_mask, -sin, 0.0)
    sin_dn = jnp.where(up_mask, 0.0, sin)
    reps = (LANES - lane_offset) // head_dim if lane_offset == 0 else 1

    def place(t, fill):
        t = jnp.tile(t, (1, reps))
        left = jnp.full((seq, lane_offset), fill, F32)
        right = jnp.zeros((seq, LANES - lane_offset - t.shape[1]), F32)
        return jnp.concatenate([left, t, right], axis=-1)

    lat = jnp.stack([place(cos, 1.0), place(sin_up, 0.0), place(sin_dn, 0.0)])
    ident = jnp.stack([jnp.ones((tm, LANES), F32), jnp.zeros((tm, LANES), F32), jnp.zeros((tm, LANES), F32)])
    if lane_offset:
        keep = (jnp.arange(LANES) < lane_offset + head_dim).astype(F32)
        ident = ident * keep
    return jnp.concatenate([ident, lat], axis=1)


def _prepare(tl, w_in, w_out, gqa_g_q, gqa_g_k, conv_w, conv_b, mla_g_cq, mla_g_ckv, mla_w_uq,
             mla_w_ukv, mla_g_qn, mla_g_kn, mla_g_qr, mla_g_kr):
    depth, d, _ = w_in.shape
    pad = MLA_IN_PAD - MLA_IN
    w_in_p = jnp.concatenate([w_in, jnp.zeros((depth, d, pad), w_in.dtype)], axis=-1).astype(BF16)
    dq = MLA_NOPE + MLA_ROPE
    wq = mla_w_uq.reshape(depth, MLA_Q_LORA, MLA_HEADS, dq)
    wq = jnp.concatenate([wq, jnp.zeros((depth, MLA_Q_LORA, MLA_HEADS, HEAD_SLOT - dq), wq.dtype)], axis=-1)
    w_uq = wq.reshape(depth, MLA_Q_LORA, MLA_HEADS * HEAD_SLOT).astype(BF16)

    def vec(a):
        return a.reshape(depth, 1, a.shape[-1]).astype(F32)

    zeros32 = jnp.zeros((depth, HEAD_SLOT - dq), F32)
    mla_scale = dq ** -0.5 * LOG2_E
    g_qm = jnp.tile(jnp.concatenate([mla_g_qn * mla_scale, mla_g_qr * mla_scale, zeros32], axis=-1),
                    (1, MLA_HEADS))
    g_km = jnp.tile(jnp.concatenate([mla_g_kn, jnp.zeros((depth, MLA_V), F32)], axis=-1), (1, MLA_HEADS))
    g_kr = jnp.concatenate([mla_g_kr, jnp.zeros((depth, LANES - MLA_ROPE), F32)], axis=-1)
    gqa_scale = GQA_HEAD_DIM ** -0.5 * LOG2_E
    g_qk = jnp.concatenate([jnp.tile(gqa_g_q * gqa_scale, (1, GQA_HEADS)),
                            jnp.tile(gqa_g_k, (1, GQA_KV_HEADS))], axis=-1)
    return {
        "w_in": w_in_p,
        "w_out": w_out.astype(BF16),
        "w_uq": w_uq,
        "w_ukv": mla_w_ukv.astype(BF16),
        "g_qk": vec(g_qk), "g_cq": vec(mla_g_cq), "g_ckv": vec(mla_g_ckv),
        "g_qm": vec(g_qm), "g_km": vec(g_km), "g_kr": vec(g_kr),
        "conv_w": conv_w.astype(F32), "conv_b": vec(conv_b),
        "bd_a": _block_diag_mean([(64, True), (64, True)]),
        "bd_q": _block_diag_mean([(MLA_NOPE, True), (MLA_ROPE, True)]),
        "bd_k": _block_diag_mean([(MLA_NOPE, True)]),
        "tab_a": _rope_tables(tl.seq, tl.tm, GQA_HEAD_DIM, 0),
        "tab_m": _rope_tables(tl.seq, tl.tm, MLA_ROPE, MLA_NOPE),
    }


GQA_K_SLOT = (0,) * N_HEADS
GQA_V_SLOT = tuple(h // (GQA_HEADS // GQA_KV_HEADS) for h in range(N_HEADS))
MLA_SLOT = tuple(range(N_HEADS))


def kernel(x, c, ctx, c_ctx, w_mod, b_mod, g_norm, ffn_w_gate, ffn_w_up, ffn_w_down, w_in, w_out,
           gqa_g_q, gqa_g_k, conv_w, conv_b, mla_g_cq, mla_g_ckv, mla_w_uq, mla_w_ukv, mla_g_qn,
           mla_g_kn, mla_g_qr, mla_g_kr):
    batch, seq, d = x.shape
    ctx_len = ctx.shape[1]
    depth = w_mod.shape[0]
    d_ff = ffn_w_gate.shape[-1]
    assert d == 1024 and batch + 1 <= MOD_ROWS and seq % GRID_W == 0
    tl = _Tiling(batch, ctx_len, seq, TOKEN_TILE)
    tl_ffn = _Tiling(batch, ctx_len, seq, FFN_TILE)
    n_all = tl.ctx_tiles + tl.lat_tiles

    cvec = jnp.concatenate([c_ctx[None, :], c, jnp.zeros((MOD_ROWS - 1 - batch, d), F32)], axis=0)
    mods = _modulation(cvec, w_mod, b_mod).reshape(depth, MOD_ROWS, N_MOD, d)
    p = _prepare(tl, w_in, w_out, gqa_g_q, gqa_g_k, conv_w, conv_b, mla_g_cq, mla_g_ckv, mla_w_uq,
                 mla_w_ukv, mla_g_qn, mla_g_kn, mla_g_qr, mla_g_kr)
    g3 = g_norm.reshape(depth * 3, 1, d)
    ffn_f32 = (ffn_w_gate.reshape(depth * 2, d, d_ff), ffn_w_up.reshape(depth * 2, d, d_ff),
               ffn_w_down.reshape(depth * 2, d_ff, d))
    ffn_w = tuple(w[0:1].astype(BF16) for w in ffn_f32)

    xs = (ctx.reshape(batch * ctx_len, d), x.reshape(batch * seq, d))
    for layer in range(depth):
        need_ctx = layer < depth - 1
        xs = _ffn(tl_ffn, xs, mods, g3, *ffn_w, layer=layer, n_tiles=tl_ffn.ctx_tiles + tl_ffn.lat_tiles)
        qa, ka, va, u, bg, qm, km, vm = _mix_pre(tl, xs, mods, g3, p, layer=layer)
        ya, ya_c, ffn_w = _attention(tl, qa, ka, va, k_slot=GQA_K_SLOT, v_slot=GQA_V_SLOT, ctx_queries=need_ctx,
                                     cast=[(w, 2 * layer + 1) for w in ffn_f32])
        ym, ym_c, ffn_w_next = _attention(tl, qm, km, vm, k_slot=MLA_SLOT, v_slot=MLA_SLOT, ctx_queries=need_ctx,
                                          cast=[(w, 2 * layer + 2) for w in ffn_f32] if need_ctx else ())
        if need_ctx:
            ya, ym = (ya_c, ya), (ym_c, ym)
            first, n_tiles = 0, n_all
        else:
            first, n_tiles = tl.ctx_tiles, tl.lat_tiles
        xs = _mix_post(tl, xs, mods, ya, ym, u, bg, p, g3, *ffn_w,
                       layer=layer, first_tile=first, n_tiles=n_tiles)
        ffn_w = ffn_w_next
    return xs.reshape(batch, seq, d)
```

```python
import functools

import jax
import jax.numpy as jnp
from jax import lax
from jax.experimental import pallas as pl
from jax.experimental.pallas import tpu as pltpu

F32 = jnp.float32
BF16 = jnp.bfloat16

GRID_W = 64
N_MOD = 9
EPS = 1e-6
ROPE_THETA = 10000.0
LOG2_E = 1.4426950408889634
GQA_HEADS = 6
GQA_KV_HEADS = 2
GQA_HEAD_DIM = 64
GQA_Q = GQA_HEADS * GQA_HEAD_DIM
GQA_KV = GQA_KV_HEADS * GQA_HEAD_DIM
GQA_IN = GQA_Q + 2 * GQA_KV
CONV_DIM = 256
CONV_WIDTH = 3
CONV_IN = 3 * CONV_DIM
MLA_HEADS = 6
MLA_Q_LORA = 384
MLA_KV_LORA = 256
MLA_NOPE = 64
MLA_ROPE = 32
MLA_V = 64
MLA_IN = MLA_Q_LORA + MLA_KV_LORA + MLA_ROPE

LANES = 128
SUBLANES = 8
MXU_DIM = 256
HEAD_SLOT = LANES
N_HEADS = 6
MLA_IN_PAD = 768
MOD_ROWS = 16
VMEM_LIMIT = 56 * 1024 * 1024
TOKEN_TILE = 512
FFN_TILE = 1024
Q_TILE = 2048
Q_SUB = 1024
FF_CHUNK = 512


def _dot(a, b):
    return jnp.dot(a, b, preferred_element_type=F32)


def _dot_nt(a, b):
    return lax.dot_general(a, b, (((1,), (1,)), ((), ())), preferred_element_type=F32)


def _silu(x):
    return x * jax.nn.sigmoid(x)


def _row_rms_scale(x, width):
    return lax.rsqrt(jnp.sum(x * x, axis=-1, keepdims=True) * (1.0 / width) + EPS)


def _norm_mod(x, g, shift, scale):
    y = x * _row_rms_scale(x, x.shape[-1])
    return (y * g) * (1.0 + scale) + shift


def _group_mean_sq(x, bd_ref):
    cols = []
    bd = bd_ref[...]
    for c in range(0, x.shape[-1], MXU_DIM):
        sq = x[:, c:c + MXU_DIM]
        cols.append(_dot((sq * sq).astype(BF16), bd))
    return jnp.concatenate(cols, axis=-1)


def _rope_cols(x, tab_ref, quarter):
    cos, sin_up, sin_dn = tab_ref[0], tab_ref[1], tab_ref[2]
    cols = []
    for c in range(0, x.shape[-1], LANES):
        v = x[:, c:c + LANES]
        up = pltpu.roll(v, LANES - quarter, axis=1)
        dn = pltpu.roll(v, quarter, axis=1)
        cols.append(v * cos + up * sin_up + dn * sin_dn)
    return cols


def _mod_kernel(c_ref, w_ref, b_ref, o_ref):
    s = _silu(c_ref[...]).astype(BF16)
    o_ref[...] = _dot(s, w_ref[...].astype(BF16)) + b_ref[...]


def _modulation(cvec, w_mod, b_mod):
    depth, d, n = w_mod.shape
    tn = 1024
    return pl.pallas_call(
        _mod_kernel,
        out_shape=jax.ShapeDtypeStruct((depth, MOD_ROWS, n), F32),
        grid=(depth, n // tn),
        in_specs=[
            pl.BlockSpec((MOD_ROWS, d), lambda l, j: (0, 0)),
            pl.BlockSpec((None, d, tn), lambda l, j: (l, 0, j)),
            pl.BlockSpec((None, 1, tn), lambda l, j: (l, 0, j)),
        ],
        out_specs=pl.BlockSpec((None, MOD_ROWS, tn), lambda l, j: (l, 0, j)),
        compiler_params=pltpu.CompilerParams(
            dimension_semantics=("arbitrary", "arbitrary"), vmem_limit_bytes=VMEM_LIMIT),
        name="modulation",
    )(cvec, w_mod, b_mod.reshape(depth, 1, n))


class _Tiling:
    def __init__(self, batch, ctx_len, seq, tm):
        self.batch, self.ctx_len, self.seq = batch, ctx_len, seq
        self.n_ctx_rows = batch * ctx_len
        self.n_rows = self.n_ctx_rows + batch * seq
        while self.n_ctx_rows % tm or seq % tm:
            tm //= 2
        assert tm >= SUBLANES
        self.tm = tm
        self.ctx_tiles = self.n_ctx_rows // tm
        self.lat_tiles = batch * seq // tm
        self.tiles_per_seq = seq // tm
        assert self.n_ctx_rows % seq == 0

    def mod_row(self, i):
        return jnp.where(i < self.ctx_tiles, 0, 1 + (i - self.ctx_tiles) // self.tiles_per_seq)

    def rope_block(self, i):
        return jnp.where(i < self.ctx_tiles, 0, 1 + (i - self.ctx_tiles) % self.tiles_per_seq)


def _mod_spec(tl, layer, first_tile, d):
    return pl.BlockSpec((None, None, N_MOD, d), lambda i: (layer, tl.mod_row(i + first_tile), 0, 0))


def _layer_vec_spec(layer, width):
    return pl.BlockSpec((None, 1, width), lambda i: (layer, 0, 0))


def _resident(shape, layer):
    nd = len(shape)
    return pl.BlockSpec((None,) + tuple(shape), lambda i: (layer,) + (0,) * nd,
                        pipeline_mode=pl.Buffered(1))


def _ffn_body(x, g, mod_ref, base, wg_ref, wu_ref, wd_ref):
    h = _norm_mod(x, g, mod_ref[base:base + 1, :], mod_ref[base + 1:base + 2, :]).astype(BF16)
    d_ff = wg_ref.shape[-1]
    acc = jnp.zeros(x.shape, F32)
    for c0 in range(0, d_ff, FF_CHUNK):
        c1 = min(c0 + FF_CHUNK, d_ff)
        gate = _dot(h, wg_ref[:, c0:c1])
        up = _dot(h, wu_ref[:, c0:c1])
        act = (_silu(gate) * up).astype(BF16)
        acc = acc + _dot(act, wd_ref[c0:c1, :])
    return x + (0.5 * mod_ref[base + 2:base + 3, :]) * acc


def _rows_specs(tl, rows, width, first_tile):
    tm = tl.tm
    if not isinstance(rows, tuple):
        return [pl.BlockSpec((tm, width), lambda i: (i, 0))]
    assert first_tile == 0
    return [pl.BlockSpec((tm, width), lambda i: (jnp.minimum(i, tl.ctx_tiles - 1), 0)),
            pl.BlockSpec((tm, width), lambda i: (jnp.maximum(i - tl.ctx_tiles, 0), 0))]


def _rows_load(tl, refs):
    if len(refs) == 1:
        return refs[0][...]
    return jnp.where(pl.program_id(0) < tl.ctx_tiles, refs[0][...], refs[1][...])


def _rows_args(rows):
    return rows if isinstance(rows, tuple) else (rows,)


def _ffn_kernel(*refs, base, tl, n_x):
    g_ref, mod_ref, wg_ref, wu_ref, wd_ref, o_ref = refs[n_x:]
    o_ref[...] = _ffn_body(_rows_load(tl, refs[:n_x]), g_ref[...], mod_ref, base, wg_ref, wu_ref, wd_ref)


def _ffn(tl, xs, mods, g_norm, wg, wu, wd, *, layer, n_tiles):
    d, d_ff = wg.shape[-2:]
    base, gi, wi, first_tile = 0, layer * 3, 0, 0
    x_args = _rows_args(xs)
    return pl.pallas_call(
        functools.partial(_ffn_kernel, base=base, tl=tl, n_x=len(x_args)),
        out_shape=jax.ShapeDtypeStruct((n_tiles * tl.tm, d), F32),
        grid=(n_tiles,),
        in_specs=_rows_specs(tl, xs, d, first_tile) + [
            _layer_vec_spec(gi, d),
            _mod_spec(tl, layer, first_tile, d),
            _resident((d, d_ff), wi),
            _resident((d, d_ff), wi),
            _resident((d_ff, d), wi),
        ],
        out_specs=pl.BlockSpec((tl.tm, d), lambda i: (i, 0)),
        compiler_params=pltpu.CompilerParams(
            dimension_semantics=("arbitrary",), vmem_limit_bytes=VMEM_LIMIT),
        name="ffn",
    )(*x_args, g_norm, mods, wg, wu, wd)


def _mix_pre_kernel(x_ref, g_ref, mod_ref, w_in_ref, gqk_ref, gcq_ref, gckv_ref, wuq_ref, wukv_ref,
                    gqm_ref, gkm_ref, gkr_ref, bd_a_ref, bd_q_ref, bd_k_ref, tab_a_ref, tab_m_ref,
                    qa_ref, ka_ref, va_ref, u_ref, bg_ref, qm_ref, km_ref, vm_ref):
    x = x_ref[...]
    h = _norm_mod(x, g_ref[...], mod_ref[3:4, :], mod_ref[4:5, :]).astype(BF16)
    lane = lax.broadcasted_iota(jnp.int32, (x.shape[0], LANES), 1)
    low = lane < GQA_HEAD_DIM
    one_hot0 = (lane == 0).astype(F32)

    pa = _dot(h, w_in_ref[:, 0:GQA_IN])
    qk = pa[:, 0:GQA_Q + GQA_KV]
    qk = qk * lax.rsqrt(_group_mean_sq(qk, bd_a_ref) + EPS) * gqk_ref[...]
    c0, c1, c2, kcol = _rope_cols(qk, tab_a_ref, GQA_HEAD_DIM // 4)
    zero = jnp.zeros_like(c0)
    slots = [
        jnp.where(low, c0, zero),
        jnp.where(low, pltpu.roll(c0, GQA_HEAD_DIM, axis=1), zero),
        jnp.where(low, c1, zero),
        jnp.where(low, zero, c1),
        jnp.where(low, zero, pltpu.roll(c2, GQA_HEAD_DIM, axis=1)),
        jnp.where(low, zero, c2),
    ]
    qa_ref[...] = jnp.concatenate(slots, axis=-1).astype(BF16)
    ka_ref[...] = kcol.astype(BF16)
    vcol = pa[:, GQA_Q + GQA_KV:GQA_IN]
    va_ref[...] = jnp.concatenate(
        [jnp.where(low, one_hot0, pltpu.roll(vcol, GQA_HEAD_DIM, axis=1)),
         jnp.where(low, one_hot0, vcol)], axis=-1).T.astype(BF16)

    ps = _dot(h, w_in_ref[:, GQA_IN:GQA_IN + CONV_IN])
    u_ref[...] = ps[:, 2 * CONV_DIM:3 * CONV_DIM] * ps[:, 0:CONV_DIM]
    bg_ref[...] = ps[:, CONV_DIM:2 * CONV_DIM]

    pm = _dot(h, w_in_ref[:, GQA_IN + CONV_IN:GQA_IN + CONV_IN + MLA_IN_PAD])
    cq = pm[:, 0:MLA_Q_LORA]
    cq = (cq * _row_rms_scale(cq, MLA_Q_LORA) * gcq_ref[...]).astype(BF16)
    ckv = pm[:, MLA_Q_LORA:MLA_Q_LORA + MLA_KV_LORA]
    ckv = (ckv * _row_rms_scale(ckv, MLA_KV_LORA) * gckv_ref[...]).astype(BF16)
    q = _dot(cq, wuq_ref[...])
    q = q * lax.rsqrt(_group_mean_sq(q, bd_q_ref) + EPS) * gqm_ref[...]
    qm_ref[...] = jnp.concatenate(_rope_cols(q, tab_m_ref, MLA_ROPE // 4), axis=-1).astype(BF16)
    kr = pm[:, MLA_Q_LORA + MLA_KV_LORA:MLA_IN_PAD]
    kr = kr * _row_rms_scale(kr, MLA_ROPE) * gkr_ref[...]
    kr = _rope_cols(pltpu.roll(kr, MLA_NOPE, axis=1), tab_m_ref, MLA_ROPE // 4)[0]
    kv = _dot(ckv, wukv_ref[...])
    kn = kv * lax.rsqrt(_group_mean_sq(kv, bd_k_ref) + EPS) * gkm_ref[...]
    km_ref[...] = jnp.concatenate(
        [jnp.where(low, kn[:, c:c + LANES], kr) for c in range(0, kn.shape[-1], LANES)],
        axis=-1).astype(BF16)
    vm_ref[...] = jnp.concatenate(
        [jnp.where(low, one_hot0, kv[:, c:c + LANES]) for c in range(0, kv.shape[-1], LANES)],
        axis=-1).T.astype(BF16)


def _mix_pre(tl, xs, mods, g_norm, p, *, layer):
    d = xs.shape[-1]
    tm = tl.tm
    n_tiles = tl.ctx_tiles + tl.lat_tiles
    t = tl.n_rows
    hs = N_HEADS * HEAD_SLOT

    def const2(shape):
        return pl.BlockSpec(shape, lambda i: (0, 0))

    def rope_spec():
        return pl.BlockSpec((3, tm, LANES), lambda i: (0, tl.rope_block(i), 0))

    def out(width):
        return pl.BlockSpec((tm, width), lambda i: (i, 0))

    def out_t(width):
        return pl.BlockSpec((width, tm), lambda i: (0, i))

    w_in = p["w_in"]
    return pl.pallas_call(
        _mix_pre_kernel,
        out_shape=(
            jax.ShapeDtypeStruct((t, hs), BF16),
            jax.ShapeDtypeStruct((t, LANES), BF16),
            jax.ShapeDtypeStruct((2 * LANES, t), BF16),
            jax.ShapeDtypeStruct((t, CONV_DIM), F32),
            jax.ShapeDtypeStruct((t, CONV_DIM), F32),
            jax.ShapeDtypeStruct((t, hs), BF16),
            jax.ShapeDtypeStruct((t, hs), BF16),
            jax.ShapeDtypeStruct((hs, t), BF16),
        ),
        grid=(n_tiles,),
        in_specs=[
            pl.BlockSpec((tm, d), lambda i: (i, 0)),
            _layer_vec_spec(layer * 3 + 1, d),
            _mod_spec(tl, layer, 0, d),
            _resident(w_in.shape[1:], layer),
            _layer_vec_spec(layer, GQA_Q + GQA_KV),
            _layer_vec_spec(layer, MLA_Q_LORA),
            _layer_vec_spec(layer, MLA_KV_LORA),
            _resident(p["w_uq"].shape[1:], layer),
            _resident(p["w_ukv"].shape[1:], layer),
            _layer_vec_spec(layer, hs),
            _layer_vec_spec(layer, hs),
            _layer_vec_spec(layer, LANES),
            const2((MXU_DIM, MXU_DIM)),
            const2((MXU_DIM, MXU_DIM)),
            const2((MXU_DIM, MXU_DIM)),
            rope_spec(),
            rope_spec(),
        ],
        out_specs=(out(hs), out(LANES), out_t(2 * LANES), out(CONV_DIM), out(CONV_DIM),
                   out(hs), out(hs), out_t(hs)),
        compiler_params=pltpu.CompilerParams(
            dimension_semantics=("arbitrary",), vmem_limit_bytes=VMEM_LIMIT),
        name="mix_pre",
    )(xs, g_norm, mods, w_in, p["g_qk"], p["g_cq"], p["g_ckv"], p["w_uq"], p["w_ukv"],
      p["g_qm"], p["g_km"], p["g_kr"], p["bd_a"], p["bd_q"], p["bd_k"], p["tab_a"], p["tab_m"])


def _attend(jobs, k_slot, v_slot):
    units = []
    for job in jobs:
        sub = min(Q_SUB, job[0].shape[0])
        units += [(job, r, sub, h) for r in range(0, job[0].shape[0], sub) for h in range(N_HEADS)]

    def scores(unit):
        (q_ref, k_refs, _, _), r, sub, h = unit
        q = q_ref[r:r + sub, h * HEAD_SLOT:(h + 1) * HEAD_SLOT]
        ks = k_slot[h] * HEAD_SLOT
        return [_dot_nt(k_ref[:, ks:ks + HEAD_SLOT], q) for k_ref in k_refs]

    def weighted_values(unit, st):
        vt_refs = unit[0][2]
        vs = v_slot[unit[3]] * HEAD_SLOT
        m = jnp.max(st[0], axis=0, keepdims=True)
        for sp in st[1:]:
            m = jnp.maximum(m, jnp.max(sp, axis=0, keepdims=True))
        acc = None
        for sp, vt_ref in zip(st, vt_refs):
            pv = _dot(vt_ref[vs:vs + HEAD_SLOT, :], jnp.exp2(sp - m).astype(BF16))
            acc = pv if acc is None else acc + pv
        return acc[MLA_V:, :] / acc[0:1, :]

    outs = []
    st = scores(units[0])
    for n, unit in enumerate(units):
        st_next = scores(units[n + 1]) if n + 1 < len(units) else None
        outs.append(weighted_values(unit, st))
        st = st_next
        (_, _, _, o_ref), r, sub, h = unit
        if h == N_HEADS - 1:
            o_ref[r:r + sub, :] = jnp.concatenate(outs, axis=0).T.astype(BF16)
            outs = []


def _attn_kernel(*refs, nq, ctx_queries, n_cast, k_slot, v_slot):
    n_in = (6 if ctx_queries else 5) + n_cast
    cast_in, cast_out = refs[n_in - n_cast:n_in], refs[len(refs) - n_cast:]
    for w_ref, wb_ref in zip(cast_in, cast_out):
        wb_ref[...] = w_ref[...].astype(BF16)
    if ctx_queries:
        q_ref, qc_ref, kc_ref, vtc_ref, kl_ref, vtl_ref = refs[:6]
        o_ref, oc_ref = refs[n_in:n_in + 2]
    else:
        q_ref, kc_ref, vtc_ref, kl_ref, vtl_ref = refs[:5]
        o_ref = refs[n_in]
    lat_job = (q_ref, (kc_ref, kl_ref), (vtc_ref, vtl_ref), o_ref)
    if not ctx_queries:
        _attend([lat_job], k_slot, v_slot)
        return
    ctx_job = (qc_ref, (kc_ref,), (vtc_ref,), oc_ref)
    if nq == 1:
        _attend([ctx_job, lat_job], k_slot, v_slot)
        return
    step = pl.program_id(1)

    @pl.when(step == 0)
    def _():
        _attend([ctx_job, lat_job], k_slot, v_slot)

    @pl.when(step > 0)
    def _():
        _attend([lat_job], k_slot, v_slot)


def _attention(tl, q, k, v, *, k_slot, v_slot, ctx_queries, cast=()):
    b, ctx_len, seq = tl.batch, tl.ctx_len, tl.seq
    kw, vw = k.shape[-1], v.shape[0]
    lat_block0 = tl.n_ctx_rows // seq
    tq = min(Q_TILE, seq)
    nq = seq // tq
    q_block0 = tl.n_ctx_rows // tq
    qw, ow = N_HEADS * HEAD_SLOT, N_HEADS * MLA_V

    def lat_tile(bi, qi):
        return bi * nq + qi

    q_specs = [pl.BlockSpec((tq, qw), lambda bi, qi: (q_block0 + lat_tile(bi, qi), 0))]
    out_specs = [pl.BlockSpec((tq, ow), lambda bi, qi: (lat_tile(bi, qi), 0))]
    out_shape = [jax.ShapeDtypeStruct((b * seq, ow), BF16)]
    operands = [q]
    if ctx_queries:
        q_specs.append(pl.BlockSpec((ctx_len, qw), lambda bi, qi: (bi, 0)))
        out_specs.append(pl.BlockSpec((ctx_len, ow), lambda bi, qi: (bi, 0)))
        out_shape.append(jax.ShapeDtypeStruct((b * ctx_len, ow), BF16))
        operands.append(q)
    kv_specs = [
        pl.BlockSpec((ctx_len, kw), lambda bi, qi: (bi, 0)),
        pl.BlockSpec((vw, ctx_len), lambda bi, qi: (0, bi)),
        pl.BlockSpec((seq, kw), lambda bi, qi: (lat_block0 + bi, 0)),
        pl.BlockSpec((vw, seq), lambda bi, qi: (0, lat_block0 + bi)),
    ]
    steps = nq
    cast_in, cast_out = [], []
    for w, layer in cast:
        rows, cols = w.shape[1:]
        chunk = rows // (b * steps)
        assert chunk * b * steps == rows and chunk % (2 * SUBLANES) == 0
        cast_in.append(pl.BlockSpec((None, chunk, cols), lambda bi, qi, layer=layer: (layer, bi * steps + qi, 0)))
        cast_out.append(pl.BlockSpec((None, chunk, cols), lambda bi, qi: (0, bi * steps + qi, 0)))
        out_shape.append(jax.ShapeDtypeStruct((1, rows, cols), BF16))
    outs = pl.pallas_call(
        functools.partial(_attn_kernel, nq=nq, ctx_queries=ctx_queries, n_cast=len(cast),
                          k_slot=k_slot, v_slot=v_slot),
        out_shape=out_shape,
        grid=(b, steps),
        in_specs=q_specs + kv_specs + cast_in,
        out_specs=out_specs + cast_out,
        compiler_params=pltpu.CompilerParams(
            dimension_semantics=("arbitrary", "arbitrary"), vmem_limit_bytes=VMEM_LIMIT),
        name="attention",
    )(*operands, k, v, k, v, *[w for w, _ in cast])
    n_o = 2 if ctx_queries else 1
    return outs[0], (outs[1] if ctx_queries else None), tuple(outs[n_o:])


def _mod_const(v, n):
    return v & (n - 1) if n & (n - 1) == 0 else lax.rem(v, n)

def _mix_post_kernel(*refs, tl, first_tile, n_y):
    ya_refs, ym_refs = refs[:n_y], refs[n_y:2 * n_y]
    (x_ref, mod_ref, u_ref, up_ref, un_ref, bg_ref, cw_ref, cb_ref, w_out_ref,
     g_ref, wg_ref, wu_ref, wd_ref, o_ref) = refs[2 * n_y:]
    tm = x_ref.shape[0]
    i = pl.program_id(0) + first_tile
    is_ctx = i < tl.ctx_tiles
    seq_len = jnp.where(is_ctx, tl.ctx_len, tl.seq)
    local = lax.broadcasted_iota(jnp.int32, (tm, CONV_DIM), 0)
    row = local + i * tm
    pos = jnp.where(is_ctx, _mod_const(row, tl.ctx_len), _mod_const(row, tl.seq))
    u = u_ref[...]
    prev = jnp.where(local == 0, up_ref[SUBLANES - 1:SUBLANES, :], pltpu.roll(u, 1, axis=0))
    prev = jnp.where(pos == 0, 0.0, prev)
    nxt = jnp.where(local == tm - 1, un_ref[0:1, :], pltpu.roll(u, tm - 1, axis=0))
    nxt = jnp.where(pos == seq_len - 1, 0.0, nxt)
    y = prev * cw_ref[0:1, :] + u * cw_ref[1:2, :] + nxt * cw_ref[2:3, :] + cb_ref[...]
    yc = (bg_ref[...] * y).astype(BF16)
    mixed = jnp.concatenate([_rows_load(tl, ya_refs), yc, _rows_load(tl, ym_refs)], axis=-1)
    x = x_ref[...] + mod_ref[5:6, :] * _dot(mixed, w_out_ref[...])
    o_ref[...] = _ffn_body(x, g_ref[...], mod_ref, 6, wg_ref, wu_ref, wd_ref)


def _mix_post(tl, xs, mods, ya, ym, u, bg, p, g_norm, wg, wu, wd, *, layer, first_tile, n_tiles):
    d = xs.shape[-1]
    d_ff = wg.shape[-1]
    tm = tl.tm
    rows8 = tm // SUBLANES
    last8 = tl.n_rows // SUBLANES - 1

    def tile(width):
        return pl.BlockSpec((tm, width), lambda i: (i + first_tile, 0))

    ya_args, ym_args = _rows_args(ya), _rows_args(ym)
    return pl.pallas_call(
        functools.partial(_mix_post_kernel, tl=tl, first_tile=first_tile, n_y=len(ya_args)),
        out_shape=jax.ShapeDtypeStruct((n_tiles * tm, d), F32),
        grid=(n_tiles,),
        in_specs=_rows_specs(tl, ya, GQA_Q, first_tile) + _rows_specs(tl, ym, N_HEADS * MLA_V, first_tile) + [
            tile(d),
            _mod_spec(tl, layer, first_tile, d),
            tile(CONV_DIM),
            pl.BlockSpec((SUBLANES, CONV_DIM),
                         lambda i: (jnp.maximum((i + first_tile) * rows8 - 1, 0), 0)),
            pl.BlockSpec((SUBLANES, CONV_DIM),
                         lambda i: (jnp.minimum((i + first_tile + 1) * rows8, last8), 0)),
            tile(CONV_DIM),
            pl.BlockSpec((None, CONV_WIDTH, CONV_DIM), lambda i: (layer, 0, 0)),
            _layer_vec_spec(layer, CONV_DIM),
            _resident(p["w_out"].shape[1:], layer),
            _layer_vec_spec(layer * 3 + 2, d),
            _resident((d, d_ff), 0),
            _resident((d, d_ff), 0),
            _resident((d_ff, d), 0),
        ],
        out_specs=pl.BlockSpec((tm, d), lambda i: (i, 0)),
        compiler_params=pltpu.CompilerParams(
            dimension_semantics=("arbitrary",), vmem_limit_bytes=VMEM_LIMIT),
        name="mix_post",
    )(*ya_args, *ym_args, xs, mods, u, u, u, bg, p["conv_w"], p["conv_b"], p["w_out"],
      g_norm, wg, wu, wd)


def _block_diag_mean(pattern):
    m = jnp.zeros((MXU_DIM, MXU_DIM), F32)
    for base in range(0, MXU_DIM, LANES):
        o = base
        for width, is_group in pattern:
            if is_group:
                m = m.at[o:o + width, o:o + width].set(1.0 / width)
            o += width
    return m.astype(BF16)


def _rope_tables(seq, tm, head_dim, lane_offset):
    rows = seq // GRID_W
    row = jnp.repeat(jnp.arange(rows, dtype=F32), GRID_W)
    col = jnp.tile(jnp.arange(GRID_W, dtype=F32), rows)
    half = head_dim // 2
    inv = 1.0 / (ROPE_THETA ** (jnp.arange(0, half, 2, dtype=F32) / half))
    ar = row[:, None] * inv[None, :]
    ac = col[:, None] * inv[None, :]
    ang = jnp.concatenate([ar, ar, ac, ac], axis=-1)
    cos, sin = jnp.cos(ang), jnp.sin(ang)
    quarter = head_dim // 4
    up_mask = (jnp.arange(head_dim) // quarter) % 2 == 0
    sin_up = jnp.where(up_mask, -sin, 0.0)
    sin_dn = jnp.where(up_mask, 0.0, sin)
    reps = (LANES - lane_offset) // head_dim if lane_offset == 0 else 1

    def place(t, fill):
        t = jnp.tile(t, (1, reps))
        left = jnp.full((seq, lane_offset), fill, F32)
        right = jnp.zeros((seq, LANES - lane_offset - t.shape[1]), F32)
        return jnp.concatenate([left, t, right], axis=-1)

    lat = jnp.stack([place(cos, 1.0), place(sin_up, 0.0), place(sin_dn, 0.0)])
    ident = jnp.stack([jnp.ones((tm, LANES), F32), jnp.zeros((tm, LANES), F32), jnp.zeros((tm, LANES), F32)])
    if lane_offset:
        keep = (jnp.arange(LANES) < lane_offset + head_dim).astype(F32)
        ident = ident * keep
    return jnp.concatenate([ident, lat], axis=1)


def _prepare(tl, w_in, w_out, gqa_g_q, gqa_g_k, conv_w, conv_b, mla_g_cq, mla_g_ckv, mla_w_uq,
             mla_w_ukv, mla_g_qn, mla_g_kn, mla_g_qr, mla_g_kr):
    depth, d, _ = w_in.shape
    pad = MLA_IN_PAD - MLA_IN
    w_in_p = jnp.concatenate([w_in, jnp.zeros((depth, d, pad), w_in.dtype)], axis=-1).astype(BF16)
    dq = MLA_NOPE + MLA_ROPE
    wq = mla_w_uq.reshape(depth, MLA_Q_LORA, MLA_HEADS, dq)
    wq = jnp.concatenate([wq, jnp.zeros((depth, MLA_Q_LORA, MLA_HEADS, HEAD_SLOT - dq), wq.dtype)], axis=-1)
    w_uq = wq.reshape(depth, MLA_Q_LORA, MLA_HEADS * HEAD_SLOT).astype(BF16)

    def vec(a):
        return a.reshape(depth, 1, a.shape[-1]).astype(F32)

    zeros32 = jnp.zeros((depth, HEAD_SLOT - dq), F32)
    mla_scale = dq ** -0.5 * LOG2_E
    g_qm = jnp.tile(jnp.concatenate([mla_g_qn * mla_scale, mla_g_qr * mla_scale, zeros32], axis=-1),
                    (1, MLA_HEADS))
    g_km = jnp.tile(jnp.concatenate([mla_g_kn, jnp.zeros((depth, MLA_V), F32)], axis=-1), (1, MLA_HEADS))
    g_kr = jnp.concatenate([mla_g_kr, jnp.zeros((depth, LANES - MLA_ROPE), F32)], axis=-1)
    gqa_scale = GQA_HEAD_DIM ** -0.5 * LOG2_E
    g_qk = jnp.concatenate([jnp.tile(gqa_g_q * gqa_scale, (1, GQA_HEADS)),
                            jnp.tile(gqa_g_k, (1, GQA_KV_HEADS))], axis=-1)
    return {
        "w_in": w_in_p,
        "w_out": w_out.astype(BF16),
        "w_uq": w_uq,
        "w_ukv": mla_w_ukv.astype(BF16),
        "g_qk": vec(g_qk), "g_cq": vec(mla_g_cq), "g_ckv": vec(mla_g_ckv),
        "g_qm": vec(g_qm), "g_km": vec(g_km), "g_kr": vec(g_kr),
        "conv_w": conv_w.astype(F32), "conv_b": vec(conv_b),
        "bd_a": _block_diag_mean([(64, True), (64, True)]),
        "bd_q": _block_diag_mean([(MLA_NOPE, True), (MLA_ROPE, True)]),
        "bd_k": _block_diag_mean([(MLA_NOPE, True)]),
        "tab_a": _rope_tables(tl.seq, tl.tm, GQA_HEAD_DIM, 0),
        "tab_m": _rope_tables(tl.seq, tl.tm, MLA_ROPE, MLA_NOPE),
    }


GQA_K_SLOT = (0,) * N_HEADS
GQA_V_SLOT = tuple(h // (GQA_HEADS // GQA_KV_HEADS) for h in range(N_HEADS))
MLA_SLOT = tuple(range(N_HEADS))


def kernel(x, c, ctx, c_ctx, w_mod, b_mod, g_norm, ffn_w_gate, ffn_w_up, ffn_w_down, w_in, w_out,
           gqa_g_q, gqa_g_k, conv_w, conv_b, mla_g_cq, mla_g_ckv, mla_w_uq, mla_w_ukv, mla_g_qn,
           mla_g_kn, mla_g_qr, mla_g_kr):
    batch, seq, d = x.shape
    ctx_len = ctx.shape[1]
    depth = w_mod.shape[0]
    d_ff = ffn_w_gate.shape[-1]
    assert d == 1024 and batch + 1 <= MOD_ROWS and seq % GRID_W == 0
    tl = _Tiling(batch, ctx_len, seq, TOKEN_TILE)
    tl_ffn = _Tiling(batch, ctx_len, seq, FFN_TILE)
    n_all = tl.ctx_tiles + tl.lat_tiles

    cvec = jnp.concatenate([c_ctx[None, :], c, jnp.zeros((MOD_ROWS - 1 - batch, d), F32)], axis=0)
    mods = _modulation(cvec, w_mod, b_mod).reshape(depth, MOD_ROWS, N_MOD, d)
    p = _prepare(tl_ffn, w_in, w_out, gqa_g_q, gqa_g_k, conv_w, conv_b, mla_g_cq, mla_g_ckv, mla_w_uq,
                 mla_w_ukv, mla_g_qn, mla_g_kn, mla_g_qr, mla_g_kr)
    g3 = g_norm.reshape(depth * 3, 1, d)
    ffn_f32 = (ffn_w_gate.reshape(depth * 2, d, d_ff), ffn_w_up.reshape(depth * 2, d, d_ff),
               ffn_w_down.reshape(depth * 2, d_ff, d))
    ffn_w = tuple(w[0:1].astype(BF16) for w in ffn_f32)

    xs = (ctx.reshape(batch * ctx_len, d), x.reshape(batch * seq, d))
    for layer in range(depth):
        need_ctx = layer < depth - 1
        xs = _ffn(tl_ffn, xs, mods, g3, *ffn_w, layer=layer, n_tiles=tl_ffn.ctx_tiles + tl_ffn.lat_tiles)
        qa, ka, va, u, bg, qm, km, vm = _mix_pre(tl_ffn, xs, mods, g3, p, layer=layer)
        ya, ya_c, ffn_w = _attention(tl, qa, ka, va, k_slot=GQA_K_SLOT, v_slot=GQA_V_SLOT, ctx_queries=need_ctx,
                                     cast=[(w, 2 * layer + 1) for w in ffn_f32])
        ym, ym_c, ffn_w_next = _attention(tl, qm, km, vm, k_slot=MLA_SLOT, v_slot=MLA_SLOT, ctx_queries=need_ctx,
                                          cast=[(w, 2 * layer + 2) for w in ffn_f32] if need_ctx else ())
        if need_ctx:
            ya, ym = (ya_c, ya), (ym_c, ym)
            first, n_tiles = 0, n_all
        else:
            first, n_tiles = tl.ctx_tiles, tl.lat_tiles
        xs = _mix_post(tl, xs, mods, ya, ym, u, bg, p, g3, *ffn_w,
                       layer=layer, first_tile=first, n_tiles=n_tiles)
        ffn_w = ffn_w_next
    return xs.reshape(batch, seq, d)
```
